```python
import jax, jax.numpy as jnp
from jax import lax
import numpy as np

D_MODEL = 1024
BATCH = 4
SEQ = 8192
DEPTH = 1
DEC_BATCH = 32
DEC_SEQ = 64
PAST_LEN = 2048

CHUNK = 64
Q_BLOCK = 128
MIX_WIDTH = D_MODEL
SB_HEADS = 8
SB_HEAD_DIM = 64
SB_WIDTH = SB_HEADS * SB_HEAD_DIM
RET_HEADS = 4
RET_HEAD_DIM = 128
RET_WIDTH = RET_HEADS * RET_HEAD_DIM
IN_WIDTH = 3 * SB_WIDTH + 4 * RET_WIDTH
D_FF = 2816
N_MOD = 9
ROPE_BASE = 10000.0
NORM_EPS = 1e-6
MACARON_WEIGHT = 0.5

kernel_name = 'hymba_stickbreak_retention_macaron_step'


def rms_norm(x, gain):
    xf = x.astype(jnp.float32)
    inv = lax.rsqrt(jnp.mean(xf * xf, axis=-1, keepdims=True) + NORM_EPS)
    return (xf * inv * gain.astype(jnp.float32)).astype(x.dtype)


def modulate(x, gain, shift, scale):
    return rms_norm(x, gain) * (1 + scale[:, None, :]) + shift[:, None, :]


def swiglu_ffn(h, w_up, w_down):
    gate, up = jnp.split(h @ w_up, 2, axis=-1)
    return (jax.nn.silu(gate) * up) @ w_down


def rope(x, pos):
    half = x.shape[-1] // 2
    inv_freq = ROPE_BASE ** (-jnp.arange(half, dtype=jnp.float32) / half)
    ang = pos.astype(jnp.float32)[:, None] * inv_freq[None, :]
    cos = jnp.cos(ang)[None, :, None, :]
    sin = jnp.sin(ang)[None, :, None, :]
    x1, x2 = x[..., :half], x[..., half:]
    return jnp.concatenate([x1 * cos - x2 * sin, x1 * sin + x2 * cos], axis=-1)


def head_group_norm(y):
    mu = jnp.mean(y, axis=-1, keepdims=True)
    yc = y - mu
    return yc * lax.rsqrt(jnp.mean(yc * yc, axis=-1, keepdims=True) + NORM_EPS)


def retention_log_decay():
    return jnp.log1p(-jnp.exp2(-5.0 - jnp.arange(RET_HEADS, dtype=jnp.float32)))


def stick_breaking(q, k, v, q_pos, k_pos):
    z = jnp.einsum('bqhd,bkhd->bhqk', q.astype(jnp.float32), k.astype(jnp.float32)) * SB_HEAD_DIM ** -0.5
    mask = k_pos[None, :] < q_pos[:, None]
    log_rest = jnp.where(mask, jax.nn.log_sigmoid(-z), 0.0)
    between = lax.cumsum(log_rest, axis=3, reverse=True) - log_rest
    weights = jnp.where(mask, jnp.exp(jax.nn.log_sigmoid(z) + between), 0.0)
    return jnp.einsum('bhqk,bkhd->bqhd', weights, v.astype(jnp.float32))


def stick_breaking_blocked(q, k, v, pos):
    B, T, H, d = q.shape
    nb = T // Q_BLOCK
    qb = q.reshape(B, nb, Q_BLOCK, H, d).transpose(1, 0, 2, 3, 4)
    pb = pos.reshape(nb, Q_BLOCK)
    out = lax.map(lambda a: stick_breaking(a[0], k, v, a[1], pos), (qb, pb))
    return out.transpose(1, 0, 2, 3, 4).reshape(B, T, H, d)


def retention(q, k, v, s0):
    B, T, H, dk = q.shape
    dv = v.shape[-1]
    c = min(CHUNK, T)
    n = T // c
    lg = retention_log_decay()
    idx = jnp.arange(c, dtype=jnp.float32)
    diff = idx[:, None] - idx[None, :]
    intra_decay = jnp.where(diff[None] >= 0,
                            jnp.exp(jnp.maximum(diff, 0.0)[None] * lg[:, None, None]), 0.0)
    q_decay = jnp.exp((idx + 1.0)[:, None] * lg[None, :])
    k_decay = jnp.exp((c - 1.0 - idx)[:, None] * lg[None, :])
    chunk_decay = jnp.exp(c * lg)
    qc = q.reshape(B, n, c, H, dk)
    kc = k.reshape(B, n, c, H, dk)
    vc = v.reshape(B, n, c, H, dv)
    scores = jnp.einsum('bnihd,bnjhd->bnhij', qc, kc) * intra_decay
    intra = jnp.einsum('bnhij,bnjhe->bnihe', scores, vc)
    kv = jnp.einsum('bnjhd,bnjhe->nbhde', kc * k_decay[:, :, None], vc)
    q_in = (qc * q_decay[:, :, None]).transpose(1, 0, 2, 3, 4)

    def step(S, xs):
        q_n, kv_n = xs
        cross = jnp.einsum('bihd,bhde->bihe', q_n, S)
        return chunk_decay[None, :, None, None] * S + kv_n, cross

    s_final, cross = lax.scan(step, s0.astype(jnp.float32), (q_in, kv))
    out = intra + cross.transpose(1, 0, 2, 3, 4)
    return out.reshape(B, T, H, dv), s_final


def trunk_layer(x, c, pos, sb_k_past, sb_v_past, ret_s0,
                w_ada, b_ada, norm_ffn1, norm_mix, norm_ffn2,
                ffn1_w_up, ffn1_w_down, w_in, sb_q_gain, sb_k_gain, w_out,
                ffn2_w_up, ffn2_w_down):
    B, T = x.shape[0], x.shape[1]
    mod = jax.nn.silu(c) @ w_ada + b_ada
    sh1, sc1, g1, sh2, sc2, g2, sh3, sc3, g3 = jnp.split(mod, N_MOD, axis=-1)

    h = modulate(x, norm_ffn1, sh1, sc1)
    x = x + MACARON_WEIGHT * g1[:, None, :] * swiglu_ffn(h, ffn1_w_up, ffn1_w_down)

    h = modulate(x, norm_mix, sh2, sc2)
    proj = h @ w_in
    splits = [SB_WIDTH, 2 * SB_WIDTH, 3 * SB_WIDTH, 3 * SB_WIDTH + RET_WIDTH,
              3 * SB_WIDTH + 2 * RET_WIDTH, 3 * SB_WIDTH + 3 * RET_WIDTH]
    sb_q, sb_k, sb_v, r_q, r_k, r_v, r_g = jnp.split(proj, splits, axis=-1)
    sb_q = rms_norm(sb_q.reshape(B, T, SB_HEADS, SB_HEAD_DIM), sb_q_gain)
    sb_k = rms_norm(sb_k.reshape(B, T, SB_HEADS, SB_HEAD_DIM), sb_k_gain)
    sb_v = sb_v.reshape(B, T, SB_HEADS, SB_HEAD_DIM)
    r_q = rope(r_q.reshape(B, T, RET_HEADS, RET_HEAD_DIM).astype(jnp.float32), pos)
    r_k = rope(r_k.reshape(B, T, RET_HEADS, RET_HEAD_DIM).astype(jnp.float32), pos) * RET_HEAD_DIM ** -0.5
    r_v = r_v.reshape(B, T, RET_HEADS, RET_HEAD_DIM).astype(jnp.float32)

    if sb_k_past is None:
        sb_o = stick_breaking_blocked(sb_q, sb_k, sb_v, pos)
        ret_s0 = jnp.zeros((B, RET_HEADS, RET_HEAD_DIM, RET_HEAD_DIM), jnp.float32)
    else:
        k_all = jnp.concatenate([sb_k_past.astype(sb_k.dtype), sb_k], axis=1)
        v_all = jnp.concatenate([sb_v_past.astype(sb_v.dtype), sb_v], axis=1)
        k_pos = jnp.arange(k_all.shape[1], dtype=jnp.int32)
        sb_o = stick_breaking(sb_q, k_all, v_all, pos, k_pos)
    ret_o, ret_state = retention(r_q, r_k, r_v, ret_s0)

    mixed = jnp.concatenate(
        [sb_o.reshape(B, T, SB_WIDTH).astype(x.dtype),
         jax.nn.silu(r_g) * head_group_norm(ret_o).reshape(B, T, RET_WIDTH).astype(x.dtype)],
        axis=-1) @ w_out
    x = x + g2[:, None, :] * mixed

    h = modulate(x, norm_ffn2, sh3, sc3)
    x = x + MACARON_WEIGHT * g3[:, None, :] * swiglu_ffn(h, ffn2_w_up, ffn2_w_down)
    return x, sb_k, sb_v, ret_state


def setup_inputs(seed: int = 0) -> dict:
    key = jax.random.key(seed)
    ks = jax.random.split(key, 20)

    def normal(k, shape, scale):
        return jax.random.normal(k, shape, jnp.float32) * scale

    return {
        'x_prompt': normal(ks[0], (BATCH, SEQ, D_MODEL), 1.0),
        'x_sample': normal(ks[1], (DEC_BATCH, DEC_SEQ, D_MODEL), 1.0),
        'cache_sb_k': normal(ks[2], (DEPTH, DEC_BATCH, PAST_LEN, SB_HEADS, SB_HEAD_DIM), 1.0),
        'cache_sb_v': normal(ks[3], (DEPTH, DEC_BATCH, PAST_LEN, SB_HEADS, SB_HEAD_DIM), 1.0),
        'state_ret': normal(ks[4], (DEPTH, DEC_BATCH, RET_HEADS, RET_HEAD_DIM, RET_HEAD_DIM), 0.5),
        'c_prompt': normal(ks[5], (BATCH, D_MODEL), 1.0),
        'c_sample': normal(ks[6], (DEC_BATCH, D_MODEL), 1.0),
        'w_ada': normal(ks[7], (DEPTH, D_MODEL, N_MOD * D_MODEL), 0.5 * D_MODEL ** -0.5),
        'b_ada': normal(ks[8], (DEPTH, N_MOD * D_MODEL), 0.02),
        'norm_ffn1': 1.0 + normal(ks[9], (DEPTH, D_MODEL), 0.02),
        'norm_mix': 1.0 + normal(ks[10], (DEPTH, D_MODEL), 0.02),
        'norm_ffn2': 1.0 + normal(ks[11], (DEPTH, D_MODEL), 0.02),
        'ffn1_w_up': normal(ks[12], (DEPTH, D_MODEL, 2 * D_FF), D_MODEL ** -0.5),
        'ffn1_w_down': normal(ks[13], (DEPTH, D_FF, D_MODEL), D_FF ** -0.5),
        'w_in': normal(ks[14], (DEPTH, D_MODEL, IN_WIDTH), D_MODEL ** -0.5),
        'sb_q_gain': 1.0 + normal(ks[15], (DEPTH, SB_HEAD_DIM), 0.02),
        'sb_k_gain': 1.0 + normal(ks[16], (DEPTH, SB_HEAD_DIM), 0.02),
        'w_out': normal(ks[17], (DEPTH, MIX_WIDTH, D_MODEL), MIX_WIDTH ** -0.5),
        'ffn2_w_up': normal(ks[18], (DEPTH, D_MODEL, 2 * D_FF), D_MODEL ** -0.5),
        'ffn2_w_down': normal(ks[19], (DEPTH, D_FF, D_MODEL), D_FF ** -0.5),
    }


def reference(x_prompt, x_sample, cache_sb_k, cache_sb_v, state_ret, c_prompt, c_sample,
              w_ada, b_ada, norm_ffn1, norm_mix, norm_ffn2, ffn1_w_up, ffn1_w_down,
              w_in, sb_q_gain, sb_k_gain, w_out, ffn2_w_up, ffn2_w_down):
    pos_p = jnp.arange(x_prompt.shape[1], dtype=jnp.int32)
    pos_s = cache_sb_k.shape[2] + jnp.arange(x_sample.shape[1], dtype=jnp.int32)
    y_prompt, y_sample = x_prompt, x_sample
    kp_list, vp_list, sp_list, ks_list, vs_list, ss_list = [], [], [], [], [], []
    for l in range(DEPTH):
        layer_w = (w_ada[l], b_ada[l], norm_ffn1[l], norm_mix[l], norm_ffn2[l],
                   ffn1_w_up[l], ffn1_w_down[l], w_in[l], sb_q_gain[l], sb_k_gain[l],
                   w_out[l], ffn2_w_up[l], ffn2_w_down[l])
        y_prompt, kp, vp, sp = trunk_layer(y_prompt, c_prompt, pos_p, None, None, None, *layer_w)
        y_sample, ksm, vsm, ssm = trunk_layer(y_sample, c_sample, pos_s, cache_sb_k[l],
                                              cache_sb_v[l], state_ret[l], *layer_w)
        kp_list.append(kp)
        vp_list.append(vp)
        sp_list.append(sp)
        ks_list.append(ksm)
        vs_list.append(vsm)
        ss_list.append(ssm)
    new_sb_k_prompt = jnp.stack(kp_list)
    new_sb_v_prompt = jnp.stack(vp_list)
    new_ret_state_prompt = jnp.stack(sp_list)
    new_sb_k_sample = jnp.stack(ks_list)
    new_sb_v_sample = jnp.stack(vs_list)
    new_ret_state_sample = jnp.stack(ss_list)
    return (y_prompt, y_sample, new_sb_k_prompt, new_sb_v_prompt, new_ret_state_prompt,
            new_sb_k_sample, new_sb_v_sample, new_ret_state_sample)
```

```python
import functools
import math

import jax
import jax.numpy as jnp
from jax import lax
from jax.experimental import pallas as pl
from jax.experimental.pallas import tpu as pltpu

F32 = jnp.float32
BF16 = jnp.bfloat16

N_MOD = 9
NORM_EPS = 1e-6
MACARON_WEIGHT = 0.5
ROPE_BASE = 10000.0
SB_HEADS = 8
SB_HEAD_DIM = 64
SB_WIDTH = SB_HEADS * SB_HEAD_DIM
RET_HEADS = 4
RET_HEAD_DIM = 128
RET_WIDTH = RET_HEADS * RET_HEAD_DIM

LANES = 128
KEY_BLOCK = LANES
HEADS_PER_TILE = LANES // SB_HEAD_DIM
F32_EXP_ZERO = -104.0
PROMPT_RET_CHUNK = 256
VMEM_LIMIT = 56 * 1024 * 1024


def _cparams(sem):
    return pltpu.CompilerParams(dimension_semantics=sem, vmem_limit_bytes=VMEM_LIMIT)


def _resident(shape):
    nd = len(shape)
    return pl.BlockSpec(shape, lambda *_: (0,) * nd, pipeline_mode=pl.Buffered(1))


def _silu(x):
    return x * (1.0 / (1.0 + jnp.exp(-x)))


def _mod_body(c_ref, w_ref, b_ref, o_ref):
    c = c_ref[...]
    o_ref[...] = jnp.dot(_silu(c).astype(BF16), w_ref[...].astype(BF16),
                         preferred_element_type=F32) + b_ref[...]


def _modulation(c, w_ada, b_ada):
    s, d = c.shape
    n = w_ada.shape[1]
    tn = d
    return pl.pallas_call(
        _mod_body,
        grid=(n // tn,),
        in_specs=[pl.BlockSpec((s, d), lambda j: (0, 0)),
                  pl.BlockSpec((d, tn), lambda j: (0, j)),
                  pl.BlockSpec((1, tn), lambda j: (0, j))],
        out_specs=pl.BlockSpec((s, tn), lambda j: (0, j)),
        out_shape=jax.ShapeDtypeStruct((s, n), F32),
        compiler_params=_cparams(("arbitrary",)),
        name="modulation",
    )(c, w_ada, b_ada.reshape(1, n))


def _ada_rms(x, gain, shift, scale):
    inv = lax.rsqrt(jnp.mean(x * x, axis=-1, keepdims=True) + NORM_EPS)
    return (x * inv * gain) * (1.0 + scale) + shift


def _ffn_body(*refs, with_mix, n_fc):
    if with_mix:
        x_ref, sb_ref, rt_ref, mod_ref, gain_ref, wout_ref, wup_ref, wdn_ref, o_ref = refs
    else:
        x_ref, mod_ref, gain_ref, wup_ref, wdn_ref, o_ref = refs
    bs, tt, d = x_ref.shape
    ff = wdn_ref.shape[0]
    fc = ff // n_fc
    x = x_ref[...]
    mod = mod_ref[...]
    if with_mix:
        sb = sb_ref[...].reshape(bs * tt, SB_WIDTH)
        rt = rt_ref[...].reshape(bs * tt, RET_WIDTH)
        mixed = (jnp.dot(sb, wout_ref[:SB_WIDTH, :], preferred_element_type=F32)
                 + jnp.dot(rt, wout_ref[SB_WIDTH:, :], preferred_element_type=F32))
        x = x + mod[:, 5:6, :] * mixed.reshape(bs, tt, d)
        shift, scale, gate_mod = mod[:, 6:7, :], mod[:, 7:8, :], mod[:, 8:9, :]
    else:
        shift, scale, gate_mod = mod[:, 0:1, :], mod[:, 1:2, :], mod[:, 2:3, :]
    h = _ada_rms(x, gain_ref[...], shift, scale).astype(BF16).reshape(bs * tt, d)
    acc = jnp.zeros((bs * tt, d), F32)
    for c in range(n_fc):
        gate = jnp.dot(h, wup_ref[:, c * fc:(c + 1) * fc], preferred_element_type=F32)
        up = jnp.dot(h, wup_ref[:, ff + c * fc:ff + (c + 1) * fc], preferred_element_type=F32)
        act = (_silu(gate) * up).astype(BF16)
        acc = acc + jnp.dot(act, wdn_ref[c * fc:(c + 1) * fc, :], preferred_element_type=F32)
    o_ref[...] = x + MACARON_WEIGHT * gate_mod * acc.reshape(bs, tt, d)


def _ffn(x, mod, gain, w_up, w_down, *, bs, tt, mix=None):
    s, t, d = x.shape
    ff = w_down.shape[0]
    n_fc = 2 if ff % (2 * LANES) == 0 else 1
    grid = (s // bs, t // tt)
    tile = lambda w: pl.BlockSpec((bs, tt, w), lambda i, j: (i, j, 0))
    mod_spec = pl.BlockSpec((bs, N_MOD, d), lambda i, j: (i, 0, 0))
    if mix is None:
        args = (x, mod, gain, w_up, w_down)
        specs = [tile(d), mod_spec, _resident(gain.shape), _resident(w_up.shape),
                 _resident(w_down.shape)]
    else:
        sb, rt, w_out = mix
        args = (x, sb, rt, mod, gain, w_out, w_up, w_down)
        specs = [tile(d), tile(SB_WIDTH), tile(RET_WIDTH), mod_spec, _resident(gain.shape),
                 _resident(w_out.shape), _resident(w_up.shape), _resident(w_down.shape)]
    return pl.pallas_call(
        functools.partial(_ffn_body, with_mix=mix is not None, n_fc=n_fc),
        grid=grid,
        in_specs=specs,
        out_specs=tile(d),
        out_shape=jax.ShapeDtypeStruct((s, t, d), F32),
        compiler_params=_cparams(("parallel", "parallel")),
        name="ffn_mix" if mix is not None else "ffn",
    )(*args)


def _proj_body(x_ref, mod_ref, gain_ref, win_ref, qg_ref, kg_ref, cos_ref, sin_ref, bd_ref,
               q_ref, kf_ref, kb_ref, vf_ref, vb_ref, rq_ref, rk_ref, rv_ref, rg_ref):
    bs, tt, d = x_ref.shape
    m = bs * tt
    mod = mod_ref[...]
    h = _ada_rms(x_ref[...], gain_ref[...], mod[:, 3:4, :], mod[:, 4:5, :])
    h = h.astype(BF16).reshape(m, d)

    def cols(a, w):
        return jnp.dot(h, win_ref[:, a:a + w], preferred_element_type=F32)

    qk = cols(0, 2 * SB_WIDTH)
    sq = (qk * qk).astype(BF16)
    bw = bd_ref.shape[0]
    ms = jnp.concatenate(
        [jnp.dot(sq[:, j * bw:(j + 1) * bw], bd_ref[...], preferred_element_type=F32)
         for j in range(2 * SB_WIDTH // bw)], axis=1) * (1.0 / SB_HEAD_DIM)
    qk = qk * lax.rsqrt(ms + NORM_EPS)
    qn = qk[:, :SB_WIDTH] * qg_ref[...]
    kn = qk[:, SB_WIDTH:] * kg_ref[...]
    q_ref[...] = (qn * SB_HEAD_DIM ** -0.5).astype(BF16).reshape(bs, tt, SB_WIDTH)
    kf_ref[...] = kn.reshape(bs, tt, SB_WIDTH)
    kb_ref[...] = kn.astype(BF16).reshape(bs, tt, SB_WIDTH)

    v = cols(2 * SB_WIDTH, SB_WIDTH)
    vf_ref[...] = v.reshape(bs, tt, SB_WIDTH)
    vb_ref[...] = v.astype(BF16).reshape(bs, tt, SB_WIDTH)

    cos = cos_ref[...][None]
    sin = sin_ref[...][None]

    def rope(a, scale):
        y = cols(a, RET_WIDTH)
        outs = []
        for hh in range(RET_HEADS):
            yh = y[:, hh * RET_HEAD_DIM:(hh + 1) * RET_HEAD_DIM]
            sw = pltpu.roll(yh, RET_HEAD_DIM // 2, axis=1)
            r = yh.reshape(bs, tt, RET_HEAD_DIM) * cos + sw.reshape(bs, tt, RET_HEAD_DIM) * sin
            outs.append(r)
        r = jnp.concatenate(outs, axis=-1)
        if scale != 1.0:
            r = r * scale
        return r.astype(BF16)

    base = 3 * SB_WIDTH
    rq_ref[...] = rope(base, 1.0)
    rk_ref[...] = rope(base + RET_WIDTH, RET_HEAD_DIM ** -0.5)
    rv_ref[...] = cols(base + 2 * RET_WIDTH, RET_WIDTH).astype(BF16).reshape(bs, tt, RET_WIDTH)
    rg_ref[...] = _silu(cols(base + 3 * RET_WIDTH, RET_WIDTH)).astype(BF16).reshape(bs, tt, RET_WIDTH)


def _mix_proj(x, mod, gain, w_in, qg, kg, cos, sin, bd, *, bs, tt):
    s, t, d = x.shape
    grid = (s // bs, t // tt)
    tile = lambda w: pl.BlockSpec((bs, tt, w), lambda i, j: (i, j, 0))
    tab = pl.BlockSpec((tt, RET_HEAD_DIM), lambda i, j: (j, 0))
    w = SB_WIDTH
    out = lambda dt: jax.ShapeDtypeStruct((s, t, w), dt)
    return pl.pallas_call(
        _proj_body,
        grid=grid,
        in_specs=[tile(d), pl.BlockSpec((bs, N_MOD, d), lambda i, j: (i, 0, 0)),
                  _resident(gain.shape), _resident(w_in.shape), _resident(qg.shape),
                  _resident(kg.shape), tab, tab, _resident(bd.shape)],
        out_specs=[tile(w)] * 9,
        out_shape=[out(BF16), out(F32), out(BF16), out(F32), out(BF16),
                   out(BF16), out(BF16), out(BF16), out(BF16)],
        compiler_params=_cparams(("parallel", "parallel")),
        name="mix_proj",
    )(x, mod, gain, w_in, qg, kg, cos, sin, bd)


def _sb_body(*refs, rows, decode):
    if decode:
        q_ref, kn_ref, vn_ref, kp_ref, vp_ref, to_ref, o_ref, acc_ref, car_ref = refs
    else:
        q_ref, kp_ref, vp_ref, to_ref, o_ref, acc_ref, car_ref = refs
    tk = KEY_BLOCK
    q = q_ref[0]
    lane = lax.broadcasted_iota(jnp.int32, (rows, LANES), 1)
    zero = jnp.zeros_like(q)
    q2 = jnp.concatenate([jnp.where(lane < SB_HEAD_DIM, q, zero),
                          jnp.where(lane >= SB_HEAD_DIM, q, zero)], axis=0)
    tri_ones = to_ref[...]

    def sweep(kblk, vblk, first):
        z = lax.dot_general(q2, kblk, (((1,), (1,)), ((), ())), preferred_element_type=F32)
        t = jnp.log(1.0 + jnp.exp(-jnp.abs(z)))
        log_rest = jnp.minimum(-z, 0.0) - t
        log_beta = jnp.minimum(z, 0.0) - t
        if first:
            row = lax.broadcasted_iota(jnp.int32, z.shape, 0) % rows
            col = lax.broadcasted_iota(jnp.int32, z.shape, 1)
            mask = col < row
            log_rest = jnp.where(mask, log_rest, 0.0)
        hi = log_rest.astype(BF16)
        lo = (log_rest - hi.astype(F32)).astype(BF16)
        cs = (jnp.dot(hi, tri_ones, preferred_element_type=F32)
              + jnp.dot(lo, tri_ones, preferred_element_type=F32))
        between = cs[:, :tk]
        if not first:
            between = between + car_ref[...]
        w = jnp.exp(log_beta + between)
        if first:
            w = jnp.where(mask, w, 0.0)
        pv = jnp.dot(w.astype(BF16), vblk, preferred_element_type=F32)
        if first:
            acc_ref[...] = pv
            car_ref[...] = cs[:, tk:]
        else:
            acc_ref[...] += pv
            car_ref[...] += cs[:, tk:]

    def past_block(kb):
        start = pl.multiple_of(kb * tk, tk)
        return (kp_ref[0, pl.ds(start, tk), :].astype(BF16),
                vp_ref[0, pl.ds(start, tk), :].astype(BF16))

    if decode:
        pad = jnp.zeros((tk - rows, LANES), BF16)
        sweep(jnp.concatenate([kn_ref[0], pad], axis=0),
              jnp.concatenate([vn_ref[0], pad], axis=0), True)
        kb0 = kp_ref.shape[1] // tk - 1
    else:
        qi = pl.program_id(2)
        sweep(*past_block(qi), True)
        kb0 = qi - 1

    def live():
        return (jnp.max(car_ref[...]) > F32_EXP_ZERO).astype(jnp.int32)

    def cond(c):
        return jnp.logical_and(c[0] >= 0, c[1] > 0)

    def body(c):
        sweep(*past_block(c[0]), False)
        return c[0] - 1, live()

    lax.while_loop(cond, body, (kb0, live()))
    acc = acc_ref[...]
    o_ref[0] = jnp.where(lane < SB_HEAD_DIM, acc[:rows], acc[rows:]).astype(o_ref.dtype)


def _tri_ones():
    j = lax.broadcasted_iota(jnp.int32, (KEY_BLOCK, 2 * KEY_BLOCK), 0)
    s = lax.broadcasted_iota(jnp.int32, (KEY_BLOCK, 2 * KEY_BLOCK), 1)
    return jnp.logical_or(j > s, s >= KEY_BLOCK).astype(BF16)


def _sb_scratch(rows):
    return [pltpu.VMEM((HEADS_PER_TILE * rows, LANES), F32),
            pltpu.VMEM((HEADS_PER_TILE * rows, LANES), F32)]


def _stick_break_prompt(q, k, v):
    b, t, w = q.shape
    rows = KEY_BLOCK
    qspec = pl.BlockSpec((1, rows, LANES), lambda i, h, j: (i, j, h))
    full = pl.BlockSpec((1, t, LANES), lambda i, h, j: (i, 0, h))
    to = _tri_ones()
    return pl.pallas_call(
        functools.partial(_sb_body, rows=rows, decode=False),
        grid=(b, w // LANES, t // rows),
        in_specs=[qspec, full, full, pl.BlockSpec(to.shape, lambda i, h, j: (0, 0))],
        out_specs=qspec,
        out_shape=jax.ShapeDtypeStruct((b, t, w), BF16),
        scratch_shapes=_sb_scratch(rows),
        compiler_params=_cparams(("parallel", "parallel", "arbitrary")),
        name="stick_break_prompt",
    )(q, k, v, to)


def _stick_break_decode(q, k_new, v_new, k_past, v_past):
    b, rows, w = q.shape
    past = k_past.shape[1]
    new = pl.BlockSpec((1, rows, LANES), lambda i, h: (i, 0, h))
    full = pl.BlockSpec((1, past, LANES), lambda i, h: (i, 0, h))
    to = _tri_ones()
    return pl.pallas_call(
        functools.partial(_sb_body, rows=rows, decode=True),
        grid=(b, w // LANES),
        in_specs=[new, new, new, full, full, pl.BlockSpec(to.shape, lambda i, h: (0, 0))],
        out_specs=new,
        out_shape=jax.ShapeDtypeStruct((b, rows, w), BF16),
        scratch_shapes=_sb_scratch(rows),
        compiler_params=_cparams(("parallel", "parallel")),
        name="stick_break_decode",
    )(q, k_new, v_new, k_past, v_past, to)


def _ret_body(q_ref, k_ref, v_ref, g_ref, s0_ref, dm_ref, qd_ref, kd_ref, cd_ref,
              o_ref, so_ref, s_scr):
    n = pl.program_id(1)

    @pl.when(n == 0)
    def _():
        s_scr[...] = s0_ref[0]

    for h in range(RET_HEADS):
        sl = slice(h * RET_HEAD_DIM, (h + 1) * RET_HEAD_DIM)
        q = q_ref[0, :, sl]
        k = k_ref[0, :, sl]
        v = v_ref[0, :, sl]
        state = s_scr[h]
        scores = lax.dot_general(q, k, (((1,), (1,)), ((), ())),
                                 preferred_element_type=F32) * dm_ref[h]
        intra = jnp.dot(scores.astype(BF16), v, preferred_element_type=F32)
        q_in = (q.astype(F32) * qd_ref[h]).astype(BF16)
        cross = jnp.dot(q_in, state.astype(BF16), preferred_element_type=F32)
        k_out = (k.astype(F32) * kd_ref[h]).astype(BF16)
        kv = lax.dot_general(k_out, v, (((0,), (0,)), ((), ())), preferred_element_type=F32)
        s_scr[h] = cd_ref[h] * state + kv
        o = intra + cross
        oc = o - jnp.mean(o, axis=-1, keepdims=True)
        gn = oc * lax.rsqrt(jnp.mean(oc * oc, axis=-1, keepdims=True) + NORM_EPS)
        o_ref[0, :, sl] = (g_ref[0, :, sl].astype(F32) * gn).astype(o_ref.dtype)

    @pl.when(n == pl.num_programs(1) - 1)
    def _():
        so_ref[0] = s_scr[...]


def _retention_tables(c):
    lg = jnp.log1p(-jnp.exp2(-5.0 - jnp.arange(RET_HEADS, dtype=F32)))
    idx = jnp.arange(c, dtype=F32)
    diff = idx[:, None] - idx[None, :]
    dm = jnp.where(diff[None] >= 0, jnp.exp(jnp.maximum(diff, 0.0)[None] * lg[:, None, None]), 0.0)
    lanes = jnp.ones((1, 1, RET_HEAD_DIM), F32)
    qd = jnp.exp((idx + 1.0)[None, :, None] * lg[:, None, None]) * lanes
    kd = jnp.exp((c - 1.0 - idx)[None, :, None] * lg[:, None, None]) * lanes
    cd = jnp.exp(c * lg)[:, None, None] * lanes
    return dm, qd, kd, cd


def _retention(rq, rk, rv, rg, s0, chunk):
    b, t, w = rq.shape
    c = min(chunk, t)
    dm, qd, kd, cd = _retention_tables(c)
    tile = pl.BlockSpec((1, c, w), lambda i, n: (i, n, 0))
    st = pl.BlockSpec((1, RET_HEADS, RET_HEAD_DIM, RET_HEAD_DIM), lambda i, n: (i, 0, 0, 0))
    const = lambda a: pl.BlockSpec(a.shape, lambda i, n: (0,) * a.ndim)
    return pl.pallas_call(
        _ret_body,
        grid=(b, t // c),
        in_specs=[tile, tile, tile, tile, st, const(dm), const(qd), const(kd), const(cd)],
        out_specs=[tile, st],
        out_shape=[jax.ShapeDtypeStruct((b, t, w), BF16),
                   jax.ShapeDtypeStruct(s0.shape, F32)],
        scratch_shapes=[pltpu.VMEM((RET_HEADS, RET_HEAD_DIM, RET_HEAD_DIM), F32)],
        compiler_params=_cparams(("parallel", "arbitrary")),
        name="retention",
    )(rq, rk, rv, rg, s0, dm, qd, kd, cd)


def _rope_tables(pos):
    half = RET_HEAD_DIM // 2
    inv_freq = ROPE_BASE ** (-jnp.arange(half, dtype=F32) / half)
    ang = pos.astype(F32)[:, None] * inv_freq[None, :]
    cos, sin = jnp.cos(ang), jnp.sin(ang)
    return jnp.concatenate([cos, cos], axis=-1), jnp.concatenate([-sin, sin], axis=-1)


def _block_diag_ones(width):
    i = lax.broadcasted_iota(jnp.int32, (width, width), 0) // SB_HEAD_DIM
    j = lax.broadcasted_iota(jnp.int32, (width, width), 1) // SB_HEAD_DIM
    return (i == j).astype(BF16)


def _layer(x, mod, pos, past, w, *, bs, tt):
    s, t, d = x.shape
    x = _ffn(x, mod, w["norm_ffn1"], w["ffn1_w_up"], w["ffn1_w_down"], bs=bs, tt=tt)
    cos, sin = _rope_tables(pos)
    q, k_f, k_b, v_f, v_b, rq, rk, rv, rg = _mix_proj(
        x, mod, w["norm_mix"], w["w_in"], w["sb_q_gain"], w["sb_k_gain"], cos, sin,
        _block_diag_ones(2 * LANES), bs=bs, tt=tt)
    if past is None:
        sb_o = _stick_break_prompt(q, k_b, v_b)
        s0 = jnp.zeros((s, RET_HEADS, RET_HEAD_DIM, RET_HEAD_DIM), F32)
        chunk = PROMPT_RET_CHUNK
    else:
        k_past, v_past, s0 = past
        sb_o = _stick_break_decode(q, k_b, v_b,
                                   k_past.reshape(s, k_past.shape[1], SB_WIDTH),
                                   v_past.reshape(s, v_past.shape[1], SB_WIDTH))
        chunk = t
    ret_o, state = _retention(rq, rk, rv, rg, s0, chunk)
    x = _ffn(x, mod, w["norm_ffn2"], w["ffn2_w_up"], w["ffn2_w_down"], bs=bs, tt=tt,
             mix=(sb_o, ret_o, w["w_out"]))
    heads = (s, t, SB_HEADS, SB_HEAD_DIM)
    return x, k_f.reshape(heads), v_f.reshape(heads), state


def kernel(x_prompt, x_sample, cache_sb_k, cache_sb_v, state_ret, c_prompt, c_sample, w_ada, b_ada, norm_ffn1, norm_mix, norm_ffn2, ffn1_w_up, ffn1_w_down, w_in, sb_q_gain, sb_k_gain, w_out, ffn2_w_up, ffn2_w_down):
    depth = w_ada.shape[0]
    bp, tp, d = x_prompt.shape
    bd, td, _ = x_sample.shape
    past_len = cache_sb_k.shape[2]
    pos_p = jnp.arange(tp, dtype=jnp.int32)
    pos_s = past_len + jnp.arange(td, dtype=jnp.int32)
    c_all = jnp.concatenate([c_prompt, c_sample], axis=0)
    tt_p = min(512, tp)
    bs_s = max(1, min(bd, 512 // td))

    y_p, y_s = x_prompt, x_sample
    outs = [[] for _ in range(6)]
    for l in range(depth):
        w = {
            "norm_ffn1": norm_ffn1[l][None], "norm_mix": norm_mix[l][None],
            "norm_ffn2": norm_ffn2[l][None],
            "ffn1_w_up": ffn1_w_up[l].astype(BF16), "ffn1_w_down": ffn1_w_down[l].astype(BF16),
            "ffn2_w_up": ffn2_w_up[l].astype(BF16), "ffn2_w_down": ffn2_w_down[l].astype(BF16),
            "w_in": w_in[l].astype(BF16), "w_out": w_out[l].astype(BF16),
            "sb_q_gain": jnp.tile(sb_q_gain[l], SB_HEADS)[None],
            "sb_k_gain": jnp.tile(sb_k_gain[l], SB_HEADS)[None],
        }
        mod = _modulation(c_all, w_ada[l], b_ada[l]).reshape(bp + bd, N_MOD, d)
        y_p, kp, vp, sp = _layer(y_p, mod[:bp], pos_p, None, w, bs=1, tt=tt_p)
        y_s, ks, vs, ss = _layer(y_s, mod[bp:], pos_s,
                                 (cache_sb_k[l], cache_sb_v[l], state_ret[l]), w,
                                 bs=bs_s, tt=td)
        for lst, a in zip(outs, (kp, vp, sp, ks, vs, ss)):
            lst.append(a)
    return (y_p, y_s) + tuple(jnp.stack(a) for a in outs)
```

```python
import functools

import jax
import jax.numpy as jnp
from jax import lax
from jax.experimental import pallas as pl
from jax.experimental.pallas import tpu as pltpu

F32 = jnp.float32
BF16 = jnp.bfloat16

N_MOD = 9
NORM_EPS = 1e-6
MACARON_WEIGHT = 0.5
ROPE_BASE = 10000.0
SB_HEADS = 8
SB_HEAD_DIM = 64
SB_WIDTH = SB_HEADS * SB_HEAD_DIM
RET_HEADS = 4
RET_HEAD_DIM = 128
RET_WIDTH = RET_HEADS * RET_HEAD_DIM

LANES = 128
KEY_BLOCK = LANES
HEADS_PER_TILE = LANES // SB_HEAD_DIM
F32_EXP_ZERO = -104.0
PROMPT_RET_CHUNK = 256
DECODE_CHUNK = 256
VMEM_LIMIT = 56 * 1024 * 1024


def _cparams(sem):
    return pltpu.CompilerParams(dimension_semantics=sem, vmem_limit_bytes=VMEM_LIMIT)


def _resident(shape):
    nd = len(shape)
    return pl.BlockSpec(shape, lambda *_: (0,) * nd, pipeline_mode=pl.Buffered(1))


def _silu(x):
    return x * (1.0 / (1.0 + jnp.exp(-x)))


def _mod_body(c_ref, w_ref, b_ref, o_ref):
    c = c_ref[...]
    o_ref[...] = jnp.dot(_silu(c).astype(BF16), w_ref[...].astype(BF16),
                         preferred_element_type=F32) + b_ref[...]


def _modulation(c, w_ada, b_ada):
    s, d = c.shape
    n = w_ada.shape[1]
    tn = d
    return pl.pallas_call(
        _mod_body,
        grid=(n // tn,),
        in_specs=[pl.BlockSpec((s, d), lambda j: (0, 0)),
                  pl.BlockSpec((d, tn), lambda j: (0, j)),
                  pl.BlockSpec((1, tn), lambda j: (0, j))],
        out_specs=pl.BlockSpec((s, tn), lambda j: (0, j)),
        out_shape=jax.ShapeDtypeStruct((s, n), F32),
        compiler_params=_cparams(("arbitrary",)),
        name="modulation",
    )(c, w_ada, b_ada.reshape(1, n))


def _ada_rms(x, gain, shift, scale):
    inv = lax.rsqrt(jnp.mean(x * x, axis=-1, keepdims=True) + NORM_EPS)
    return (x * inv * gain) * (1.0 + scale) + shift


def _ffn_body(*refs, with_mix, n_fc):
    if with_mix:
        x_ref, sb_ref, rt_ref, mod_ref, gain_ref, wout_ref, wup_ref, wdn_ref, o_ref = refs
    else:
        x_ref, mod_ref, gain_ref, wup_ref, wdn_ref, o_ref = refs
    bs, tt, d = x_ref.shape
    ff = wdn_ref.shape[0]
    fc = ff // n_fc
    x = x_ref[...]
    mod = mod_ref[...]
    if with_mix:
        sb = sb_ref[...].reshape(bs * tt, SB_WIDTH)
        rt = rt_ref[...].reshape(bs * tt, RET_WIDTH)
        mixed = (jnp.dot(sb, wout_ref[:SB_WIDTH, :], preferred_element_type=F32)
                 + jnp.dot(rt, wout_ref[SB_WIDTH:, :], preferred_element_type=F32))
        x = x + mod[:, 5:6, :] * mixed.reshape(bs, tt, d)
        shift, scale, gate_mod = mod[:, 6:7, :], mod[:, 7:8, :], mod[:, 8:9, :]
    else:
        shift, scale, gate_mod = mod[:, 0:1, :], mod[:, 1:2, :], mod[:, 2:3, :]
    h = _ada_rms(x, gain_ref[...], shift, scale).astype(BF16).reshape(bs * tt, d)
    acc = jnp.zeros((bs * tt, d), F32)
    for c in range(n_fc):
        gate = jnp.dot(h, wup_ref[:, c * fc:(c + 1) * fc], preferred_element_type=F32)
        up = jnp.dot(h, wup_ref[:, ff + c * fc:ff + (c + 1) * fc], preferred_element_type=F32)
        act = (_silu(gate) * up).astype(BF16)
        acc = acc + jnp.dot(act, wdn_ref[c * fc:(c + 1) * fc, :], preferred_element_type=F32)
    o_ref[...] = x + MACARON_WEIGHT * gate_mod * acc.reshape(bs, tt, d)


def _ffn(x, mod, gain, w_up, w_down, *, bs, tt, mix=None):
    s, t, d = x.shape
    ff = w_down.shape[0]
    n_fc = 2 if ff % (2 * LANES) == 0 else 1
    grid = (s // bs, t // tt)
    tile = lambda w: pl.BlockSpec((bs, tt, w), lambda i, j: (i, j, 0))
    mod_spec = pl.BlockSpec((bs, N_MOD, d), lambda i, j: (i, 0, 0))
    if mix is None:
        args = (x, mod, gain, w_up, w_down)
        specs = [tile(d), mod_spec, _resident(gain.shape), _resident(w_up.shape),
                 _resident(w_down.shape)]
    else:
        sb, rt, w_out = mix
        args = (x, sb, rt, mod, gain, w_out, w_up, w_down)
        specs = [tile(d), tile(SB_WIDTH), tile(RET_WIDTH), mod_spec, _resident(gain.shape),
                 _resident(w_out.shape), _resident(w_up.shape), _resident(w_down.shape)]
    return pl.pallas_call(
        functools.partial(_ffn_body, with_mix=mix is not None, n_fc=n_fc),
        grid=grid,
        in_specs=specs,
        out_specs=tile(d),
        out_shape=jax.ShapeDtypeStruct((s, t, d), F32),
        compiler_params=_cparams(("parallel", "parallel")),
        name="ffn_mix" if mix is not None else "ffn",
    )(*args)


def _proj_body(x_ref, mod_ref, gain_ref, win_ref, qg_ref, kg_ref, cos_ref, sin_ref, bd_ref,
               q_ref, kf_ref, kb_ref, vf_ref, vb_ref, rq_ref, rk_ref, rv_ref, rg_ref):
    bs, tt, d = x_ref.shape
    m = bs * tt
    mod = mod_ref[...]
    h = _ada_rms(x_ref[...], gain_ref[...], mod[:, 3:4, :], mod[:, 4:5, :])
    h = h.astype(BF16).reshape(m, d)

    def cols(a, w):
        return jnp.dot(h, win_ref[:, a:a + w], preferred_element_type=F32)

    qk = cols(0, 2 * SB_WIDTH)
    sq = (qk * qk).astype(BF16)
    bw = bd_ref.shape[0]
    ms = jnp.concatenate(
        [jnp.dot(sq[:, j * bw:(j + 1) * bw], bd_ref[...], preferred_element_type=F32)
         for j in range(2 * SB_WIDTH // bw)], axis=1) * (1.0 / SB_HEAD_DIM)
    qk = qk * lax.rsqrt(ms + NORM_EPS)
    qn = qk[:, :SB_WIDTH] * qg_ref[...]
    kn = qk[:, SB_WIDTH:] * kg_ref[...]
    q_ref[...] = (qn * SB_HEAD_DIM ** -0.5).astype(BF16).reshape(bs, tt, SB_WIDTH)
    kf_ref[...] = kn.reshape(bs, tt, SB_WIDTH)
    kb_ref[...] = kn.astype(BF16).reshape(bs, tt, SB_WIDTH)

    v = cols(2 * SB_WIDTH, SB_WIDTH)
    vf_ref[...] = v.reshape(bs, tt, SB_WIDTH)
    vb_ref[...] = v.astype(BF16).reshape(bs, tt, SB_WIDTH)

    cos = cos_ref[...][None]
    sin = sin_ref[...][None]

    def rope(a, scale):
        y = cols(a, RET_WIDTH)
        outs = []
        for hh in range(RET_HEADS):
            yh = y[:, hh * RET_HEAD_DIM:(hh + 1) * RET_HEAD_DIM]
            sw = pltpu.roll(yh, RET_HEAD_DIM // 2, axis=1)
            r = yh.reshape(bs, tt, RET_HEAD_DIM) * cos + sw.reshape(bs, tt, RET_HEAD_DIM) * sin
            outs.append(r)
        r = jnp.concatenate(outs, axis=-1)
        if scale != 1.0:
            r = r * scale
        return r.astype(BF16)

    base = 3 * SB_WIDTH
    rq_ref[...] = rope(base, 1.0)
    rk_ref[...] = rope(base + RET_WIDTH, RET_HEAD_DIM ** -0.5)
    rv_ref[...] = cols(base + 2 * RET_WIDTH, RET_WIDTH).astype(BF16).reshape(bs, tt, RET_WIDTH)
    rg_ref[...] = _silu(cols(base + 3 * RET_WIDTH, RET_WIDTH)).astype(BF16).reshape(bs, tt, RET_WIDTH)


def _mix_proj(x, mod, gain, w_in, qg, kg, cos, sin, bd, *, bs, tt):
    s, t, d = x.shape
    grid = (s // bs, t // tt)
    tile = lambda w: pl.BlockSpec((bs, tt, w), lambda i, j: (i, j, 0))
    tab = pl.BlockSpec((tt, RET_HEAD_DIM), lambda i, j: (j, 0))
    w = SB_WIDTH
    out = lambda dt: jax.ShapeDtypeStruct((s, t, w), dt)
    return pl.pallas_call(
        _proj_body,
        grid=grid,
        in_specs=[tile(d), pl.BlockSpec((bs, N_MOD, d), lambda i, j: (i, 0, 0)),
                  _resident(gain.shape), _resident(w_in.shape), _resident(qg.shape),
                  _resident(kg.shape), tab, tab, _resident(bd.shape)],
        out_specs=[tile(w)] * 9,
        out_shape=[out(BF16), out(F32), out(BF16), out(F32), out(BF16),
                   out(BF16), out(BF16), out(BF16), out(BF16)],
        compiler_params=_cparams(("parallel", "parallel")),
        name="mix_proj",
    )(x, mod, gain, w_in, qg, kg, cos, sin, bd)


def _split_heads(blk, axis):
    head_axis = 1 - axis
    pos = lax.broadcasted_iota(jnp.int32, blk.shape, head_axis)
    zero = jnp.zeros_like(blk)
    return jnp.concatenate([jnp.where(pos < SB_HEAD_DIM, blk, zero),
                            jnp.where(pos >= SB_HEAD_DIM, blk, zero)], axis=axis)


def _sb_sweep(qs, kblks, vblks, tri_ones2, acc_ref, car_ref, *, rows, first, transposed):
    tk = KEY_BLOCK
    nt = (((1,), (1,)), ((), ()))
    zs = []
    for q, kblk in zip(qs, kblks):
        if transposed:
            z = jnp.dot(q, _split_heads(kblk, 1), preferred_element_type=F32)
        else:
            z = lax.dot_general(q, _split_heads(kblk, 0), nt, preferred_element_type=F32)
        zs += [z[:, :tk], z[:, tk:]]
    z = jnp.concatenate(zs, axis=0)
    t = jnp.log(1.0 + jnp.exp(-jnp.abs(z)))
    log_rest = jnp.minimum(-z, 0.0) - t
    log_beta = jnp.minimum(z, 0.0) - t
    if first:
        row = lax.broadcasted_iota(jnp.int32, z.shape, 0) % rows
        col = lax.broadcasted_iota(jnp.int32, z.shape, 1)
        mask = col < row
        log_rest = jnp.where(mask, log_rest, 0.0)
    hi = log_rest.astype(BF16)
    lo = (log_rest - hi.astype(F32)).astype(BF16)
    cs = jnp.dot(jnp.concatenate([hi, lo], axis=1), tri_ones2, preferred_element_type=F32)
    between = cs[:, :tk]
    if not first:
        between = between + car_ref[...]
    w = jnp.exp(log_beta + between)
    if first:
        w = jnp.where(mask, w, 0.0)
    w = w.astype(BF16)
    for p, vblk in enumerate(vblks):
        w2 = jnp.concatenate([w[(2 * p) * rows:(2 * p + 1) * rows],
                              w[(2 * p + 1) * rows:(2 * p + 2) * rows]], axis=1)
        if transposed:
            pv = lax.dot_general(w2, _split_heads(vblk, 1), nt, preferred_element_type=F32)
        else:
            pv = jnp.dot(w2, _split_heads(vblk, 0), preferred_element_type=F32)
        if first:
            acc_ref[p * rows:(p + 1) * rows, :] = pv
        else:
            acc_ref[p * rows:(p + 1) * rows, :] += pv
    if first:
        car_ref[...] = cs[:, tk:]
    else:
        car_ref[...] += cs[:, tk:]


def _sb_live(car_ref):
    return (jnp.max(car_ref[...]) > F32_EXP_ZERO).astype(jnp.int32)


def _sb_store(o_ref, acc_ref, rows, tiles):
    for p in range(tiles):
        o_ref[0, :, p * LANES:(p + 1) * LANES] = acc_ref[p * rows:(p + 1) * rows, :].astype(o_ref.dtype)


def _sb_prompt_body(q_ref, k_ref, v_ref, to_ref, o_ref, acc_ref, car_ref):
    qi = pl.program_id(1)
    rows = q_ref.shape[1]
    tk = KEY_BLOCK
    tiles = q_ref.shape[2] // LANES
    lanes = [slice(p * LANES, (p + 1) * LANES) for p in range(tiles)]
    tri_ones2 = to_ref[...]
    qs = [q_ref[0, :, sl] for sl in lanes]

    def block(kb, first):
        keys = pl.ds(pl.multiple_of(kb * tk, tk), tk)
        _sb_sweep(qs, [k_ref[0, keys, sl] for sl in lanes], [v_ref[0, keys, sl] for sl in lanes],
                  tri_ones2, acc_ref, car_ref, rows=rows, first=first, transposed=False)

    block(qi, True)

    def body(c):
        block(c[0], False)
        return c[0] - 1, _sb_live(car_ref)

    lax.while_loop(lambda c: jnp.logical_and(c[0] >= 0, c[1] > 0), body,
                   (qi - 1, _sb_live(car_ref)))
    _sb_store(o_ref, acc_ref, rows, tiles)


def _sb_decode_body(q_ref, kn_ref, vn_ref, kt_hbm, vt_hbm, to_ref, o_ref,
                    acc_ref, car_ref, kbuf, vbuf, sem):
    b = pl.program_id(0)
    rows = q_ref.shape[1]
    tk = KEY_BLOCK
    tiles = q_ref.shape[2] // LANES
    lanes = [slice(p * LANES, (p + 1) * LANES) for p in range(tiles)]
    chunk = kbuf.shape[2]
    n_chunks = kt_hbm.shape[3] // chunk
    tri_ones2 = to_ref[...]
    qs = [q_ref[0, :, sl] for sl in lanes]

    def copies(ci):
        start = pl.multiple_of(ci * chunk, chunk)
        return (pltpu.make_async_copy(kt_hbm.at[b, :, :, pl.ds(start, chunk)], kbuf, sem.at[0]),
                pltpu.make_async_copy(vt_hbm.at[b, :, :, pl.ds(start, chunk)], vbuf, sem.at[1]))

    for cp in copies(n_chunks - 1):
        cp.start()
    pad = jnp.zeros((tk - rows, LANES), BF16)
    _sb_sweep(qs, [jnp.concatenate([kn_ref[0, :, sl], pad], axis=0) for sl in lanes],
              [jnp.concatenate([vn_ref[0, :, sl], pad], axis=0) for sl in lanes],
              tri_ones2, acc_ref, car_ref, rows=rows, first=True, transposed=False)
    for cp in copies(n_chunks - 1):
        cp.wait()

    def cached_block(sub):
        def tile(buf, p):
            blk = buf[pl.ds(p * HEADS_PER_TILE, HEADS_PER_TILE), :, pl.ds(sub * tk, tk)]
            return blk.reshape(LANES, tk).astype(BF16)
        _sb_sweep(qs, [tile(kbuf, p) for p in range(tiles)], [tile(vbuf, p) for p in range(tiles)],
                  tri_ones2, acc_ref, car_ref, rows=rows, first=False, transposed=True)

    def body(c):
        ci = c[0]

        @pl.when(ci < n_chunks - 1)
        def _():
            for cp in copies(ci):
                cp.start()
            for cp in copies(ci):
                cp.wait()

        for sub in reversed(range(chunk // tk)):
            if sub == chunk // tk - 1:
                cached_block(sub)
            else:
                pl.when(_sb_live(car_ref) > 0)(functools.partial(cached_block, sub))
        return ci - 1, _sb_live(car_ref)

    lax.while_loop(lambda c: jnp.logical_and(c[0] >= 0, c[1] > 0), body,
                   (jnp.int32(n_chunks - 1), _sb_live(car_ref)))
    _sb_store(o_ref, acc_ref, rows, tiles)


def _tri_ones2():
    j = lax.broadcasted_iota(jnp.int32, (2 * KEY_BLOCK, 2 * KEY_BLOCK), 0) % KEY_BLOCK
    s = lax.broadcasted_iota(jnp.int32, (2 * KEY_BLOCK, 2 * KEY_BLOCK), 1)
    return jnp.logical_or(j > s, s >= KEY_BLOCK).astype(BF16)


def _sb_scratch(rows, width):
    tiles = width // LANES
    return [pltpu.VMEM((tiles * rows, LANES), F32),
            pltpu.VMEM((tiles * HEADS_PER_TILE * rows, LANES), F32)]


def _stick_break_prompt(q, k, v):
    b, t, w = q.shape
    rows = KEY_BLOCK
    qspec = pl.BlockSpec((1, rows, w), lambda i, j: (i, j, 0))
    full = pl.BlockSpec((1, t, w), lambda i, j: (i, 0, 0), pipeline_mode=pl.Buffered(1))
    to = _tri_ones2()
    return pl.pallas_call(
        _sb_prompt_body,
        grid=(b, t // rows),
        in_specs=[qspec, full, full, _resident(to.shape)],
        out_specs=qspec,
        out_shape=jax.ShapeDtypeStruct((b, t, w), BF16),
        scratch_shapes=_sb_scratch(rows, w),
        compiler_params=_cparams(("parallel", "arbitrary")),
        name="stick_break_prompt",
    )(q, k, v, to)


def _stick_break_decode(q, k_new, v_new, kt_past, vt_past):
    b, rows, w = q.shape
    past = kt_past.shape[3]
    chunk = min(DECODE_CHUNK, past)
    assert past % chunk == 0 and chunk % KEY_BLOCK == 0 and rows <= KEY_BLOCK
    new = pl.BlockSpec((1, rows, w), lambda i: (i, 0, 0))
    hbm = pl.BlockSpec(memory_space=pl.ANY)
    to = _tri_ones2()
    return pl.pallas_call(
        _sb_decode_body,
        grid=(b,),
        in_specs=[new, new, new, hbm, hbm, _resident(to.shape)],
        out_specs=new,
        out_shape=jax.ShapeDtypeStruct((b, rows, w), BF16),
        scratch_shapes=_sb_scratch(rows, w) + [
            pltpu.VMEM((SB_HEADS, SB_HEAD_DIM, chunk), F32),
            pltpu.VMEM((SB_HEADS, SB_HEAD_DIM, chunk), F32),
            pltpu.SemaphoreType.DMA((2,))],
        compiler_params=_cparams(("arbitrary",)),
        name="stick_break_decode",
    )(q, k_new, v_new, kt_past, vt_past, to)


def _ret_body(q_ref, k_ref, v_ref, g_ref, s0_ref, dm_ref, qd_ref, kd_ref, cd_ref,
              o_ref, so_ref, s_scr):
    n = pl.program_id(1)

    @pl.when(n == 0)
    def _():
        s_scr[...] = s0_ref[0]

    for h in range(RET_HEADS):
        sl = slice(h * RET_HEAD_DIM, (h + 1) * RET_HEAD_DIM)
        q = q_ref[0, :, sl]
        k = k_ref[0, :, sl]
        v = v_ref[0, :, sl]
        state = s_scr[h]
        scores = lax.dot_general(q, k, (((1,), (1,)), ((), ())),
                                 preferred_element_type=F32) * dm_ref[h]
        intra = jnp.dot(scores.astype(BF16), v, preferred_element_type=F32)
        q_in = (q.astype(F32) * qd_ref[h]).astype(BF16)
        cross = jnp.dot(q_in, state.astype(BF16), preferred_element_type=F32)
        k_out = (k.astype(F32) * kd_ref[h]).astype(BF16)
        kv = lax.dot_general(k_out, v, (((0,), (0,)), ((), ())), preferred_element_type=F32)
        s_scr[h] = cd_ref[h] * state + kv
        o = intra + cross
        oc = o - jnp.mean(o, axis=-1, keepdims=True)
        gn = oc * lax.rsqrt(jnp.mean(oc * oc, axis=-1, keepdims=True) + NORM_EPS)
        o_ref[0, :, sl] = (g_ref[0, :, sl].astype(F32) * gn).astype(o_ref.dtype)

    @pl.when(n == pl.num_programs(1) - 1)
    def _():
        so_ref[0] = s_scr[...]


def _retention_tables(c):
    lg = jnp.log1p(-jnp.exp2(-5.0 - jnp.arange(RET_HEADS, dtype=F32)))
    idx = jnp.arange(c, dtype=F32)
    diff = idx[:, None] - idx[None, :]
    dm = jnp.where(diff[None] >= 0, jnp.exp(jnp.maximum(diff, 0.0)[None] * lg[:, None, None]), 0.0)
    lanes = jnp.ones((1, 1, RET_HEAD_DIM), F32)
    qd = jnp.exp((idx + 1.0)[None, :, None] * lg[:, None, None]) * lanes
    kd = jnp.exp((c - 1.0 - idx)[None, :, None] * lg[:, None, None]) * lanes
    cd = jnp.exp(c * lg)[:, None, None] * lanes
    return dm, qd, kd, cd


def _retention(rq, rk, rv, rg, s0, chunk):
    b, t, w = rq.shape
    c = min(chunk, t)
    dm, qd, kd, cd = _retention_tables(c)
    tile = pl.BlockSpec((1, c, w), lambda i, n: (i, n, 0))
    st = pl.BlockSpec((1, RET_HEADS, RET_HEAD_DIM, RET_HEAD_DIM), lambda i, n: (i, 0, 0, 0))
    const = lambda a: pl.BlockSpec(a.shape, lambda i, n: (0,) * a.ndim)
    return pl.pallas_call(
        _ret_body,
        grid=(b, t // c),
        in_specs=[tile, tile, tile, tile, st, const(dm), const(qd), const(kd), const(cd)],
        out_specs=[tile, st],
        out_shape=[jax.ShapeDtypeStruct((b, t, w), BF16),
                   jax.ShapeDtypeStruct(s0.shape, F32)],
        scratch_shapes=[pltpu.VMEM((RET_HEADS, RET_HEAD_DIM, RET_HEAD_DIM), F32)],
        compiler_params=_cparams(("parallel", "arbitrary")),
        name="retention",
    )(rq, rk, rv, rg, s0, dm, qd, kd, cd)


def _rope_tables(pos):
    half = RET_HEAD_DIM // 2
    inv_freq = ROPE_BASE ** (-jnp.arange(half, dtype=F32) / half)
    ang = pos.astype(F32)[:, None] * inv_freq[None, :]
    cos, sin = jnp.cos(ang), jnp.sin(ang)
    return jnp.concatenate([cos, cos], axis=-1), jnp.concatenate([-sin, sin], axis=-1)


def _block_diag_ones(width):
    i = lax.broadcasted_iota(jnp.int32, (width, width), 0) // SB_HEAD_DIM
    j = lax.broadcasted_iota(jnp.int32, (width, width), 1) // SB_HEAD_DIM
    return (i == j).astype(BF16)


def _layer(x, mod, pos, past, w, *, bs, tt):
    s, t, d = x.shape
    x = _ffn(x, mod, w["norm_ffn1"], w["ffn1_w_up"], w["ffn1_w_down"], bs=bs, tt=tt)
    cos, sin = _rope_tables(pos)
    q, k_f, k_b, v_f, v_b, rq, rk, rv, rg = _mix_proj(
        x, mod, w["norm_mix"], w["w_in"], w["sb_q_gain"], w["sb_k_gain"], cos, sin,
        _block_diag_ones(2 * LANES), bs=bs, tt=tt)
    if past is None:
        sb_o = _stick_break_prompt(q, k_b, v_b)
        s0 = jnp.zeros((s, RET_HEADS, RET_HEAD_DIM, RET_HEAD_DIM), F32)
        chunk = PROMPT_RET_CHUNK
    else:
        k_past, v_past, s0 = past
        sb_o = _stick_break_decode(q, k_b, v_b, jnp.transpose(k_past, (0, 2, 3, 1)),
                                   jnp.transpose(v_past, (0, 2, 3, 1)))
        chunk = t
    ret_o, state = _retention(rq, rk, rv, rg, s0, chunk)
    x = _ffn(x, mod, w["norm_ffn2"], w["ffn2_w_up"], w["ffn2_w_down"], bs=bs, tt=tt,
             mix=(sb_o, ret_o, w["w_out"]))
    heads = (s, t, SB_HEADS, SB_HEAD_DIM)
    return x, k_f.reshape(heads), v_f.reshape(heads), state


def kernel(x_prompt, x_sample, cache_sb_k, cache_sb_v, state_ret, c_prompt, c_sample, w_ada, b_ada, norm_ffn1, norm_mix, norm_ffn2, ffn1_w_up, ffn1_w_down, w_in, sb_q_gain, sb_k_gain, w_out, ffn2_w_up, ffn2_w_down):
    depth = w_ada.shape[0]
    bp, tp, d = x_prompt.shape
    bd, td, _ = x_sample.shape
    past_len = cache_sb_k.shape[2]
    pos_p = jnp.arange(tp, dtype=jnp.int32)
    pos_s = past_len + jnp.arange(td, dtype=jnp.int32)
    c_all = jnp.concatenate([c_prompt, c_sample], axis=0)
    tt_p = min(512, tp)
    bs_s = max(1, min(bd, 512 // td))

    y_p, y_s = x_prompt, x_sample
    outs = [[] for _ in range(6)]
    for l in range(depth):
        w = {
            "norm_ffn1": norm_ffn1[l][None], "norm_mix": norm_mix[l][None],
            "norm_ffn2": norm_ffn2[l][None],
            "ffn1_w_up": ffn1_w_up[l].astype(BF16), "ffn1_w_down": ffn1_w_down[l].astype(BF16),
            "ffn2_w_up": ffn2_w_up[l].astype(BF16), "ffn2_w_down": ffn2_w_down[l].astype(BF16),
            "w_in": w_in[l].astype(BF16), "w_out": w_out[l].astype(BF16),
            "sb_q_gain": jnp.tile(sb_q_gain[l], SB_HEADS)[None],
            "sb_k_gain": jnp.tile(sb_k_gain[l], SB_HEADS)[None],
        }
        mod = _modulation(c_all, w_ada[l], b_ada[l]).reshape(bp + bd, N_MOD, d)
        y_p, kp, vp, sp = _layer(y_p, mod[:bp], pos_p, None, w, bs=1, tt=tt_p)
        y_s, ks, vs, ss = _layer(y_s, mod[bp:], pos_s,
                                 (cache_sb_k[l], cache_sb_v[l], state_ret[l]), w,
                                 bs=bs_s, tt=td)
        for lst, a in zip(outs, (kp, vp, sp, ks, vs, ss)):
            lst.append(a)
    return (y_p, y_s) + tuple(jnp.stack(a) for a in outs)
```

```python
import functools

import jax
import jax.numpy as jnp
from jax import lax
from jax.experimental import pallas as pl
from jax.experimental.pallas import tpu as pltpu

F32 = jnp.float32
BF16 = jnp.bfloat16

N_MOD = 9
NORM_EPS = 1e-6
MACARON_WEIGHT = 0.5
ROPE_BASE = 10000.0
SB_HEADS = 8
SB_HEAD_DIM = 64
SB_WIDTH = SB_HEADS * SB_HEAD_DIM
RET_HEADS = 4
RET_HEAD_DIM = 128
RET_WIDTH = RET_HEADS * RET_HEAD_DIM

LANES = 128
KEY_BLOCK = LANES
HEADS_PER_TILE = LANES // SB_HEAD_DIM
LOG2_E = 1.4426950408889634
F32_EXP2_ZERO = -126.0
SOFTPLUS_CLAMP = 64.0
PROMPT_RET_CHUNK = 256
DECODE_CHUNK = 256
VMEM_LIMIT = 56 * 1024 * 1024


def _cparams(sem):
    return pltpu.CompilerParams(dimension_semantics=sem, vmem_limit_bytes=VMEM_LIMIT)


def _resident(shape):
    nd = len(shape)
    return pl.BlockSpec(shape, lambda *_: (0,) * nd, pipeline_mode=pl.Buffered(1))


def _silu(x):
    return x * (1.0 / (1.0 + jnp.exp(-x)))


def _mod_body(c_ref, w_ref, b_ref, o_ref):
    c = c_ref[...]
    o_ref[...] = jnp.dot(_silu(c).astype(BF16), w_ref[...].astype(BF16),
                         preferred_element_type=F32) + b_ref[...]


def _modulation(c, w_ada, b_ada):
    s, d = c.shape
    n = w_ada.shape[1]
    tn = d
    return pl.pallas_call(
        _mod_body,
        grid=(n // tn,),
        in_specs=[pl.BlockSpec((s, d), lambda j: (0, 0)),
                  pl.BlockSpec((d, tn), lambda j: (0, j)),
                  pl.BlockSpec((1, tn), lambda j: (0, j))],
        out_specs=pl.BlockSpec((s, tn), lambda j: (0, j)),
        out_shape=jax.ShapeDtypeStruct((s, n), F32),
        compiler_params=_cparams(("arbitrary",)),
        name="modulation",
    )(c, w_ada, b_ada.reshape(1, n))


def _ada_rms(x, gain, shift, scale):
    inv = lax.rsqrt(jnp.mean(x * x, axis=-1, keepdims=True) + NORM_EPS)
    return (x * inv * gain) * (1.0 + scale) + shift


def _ffn_body(*refs, with_mix, n_fc):
    if with_mix:
        x_ref, sb_ref, rt_ref, mod_ref, gain_ref, wout_ref, wup_ref, wdn_ref, o_ref = refs
    else:
        x_ref, mod_ref, gain_ref, wup_ref, wdn_ref, o_ref = refs
    bs, tt, d = x_ref.shape
    ff = wdn_ref.shape[0]
    fc = ff // n_fc
    x = x_ref[...]
    mod = mod_ref[...]
    if with_mix:
        sb = sb_ref[...].reshape(bs * tt, SB_WIDTH)
        rt = rt_ref[...].reshape(bs * tt, RET_WIDTH)
        mixed = (jnp.dot(sb, wout_ref[:SB_WIDTH, :], preferred_element_type=F32)
                 + jnp.dot(rt, wout_ref[SB_WIDTH:, :], preferred_element_type=F32))
        x = x + mod[:, 5:6, :] * mixed.reshape(bs, tt, d)
        shift, scale, gate_mod = mod[:, 6:7, :], mod[:, 7:8, :], mod[:, 8:9, :]
    else:
        shift, scale, gate_mod = mod[:, 0:1, :], mod[:, 1:2, :], mod[:, 2:3, :]
    h = _ada_rms(x, gain_ref[...], shift, scale).astype(BF16).reshape(bs * tt, d)
    acc = jnp.zeros((bs * tt, d), F32)
    for c in range(n_fc):
        gate = jnp.dot(h, wup_ref[:, c * fc:(c + 1) * fc], preferred_element_type=F32)
        up = jnp.dot(h, wup_ref[:, ff + c * fc:ff + (c + 1) * fc], preferred_element_type=F32)
        act = (_silu(gate) * up).astype(BF16)
        acc = acc + jnp.dot(act, wdn_ref[c * fc:(c + 1) * fc, :], preferred_element_type=F32)
    o_ref[...] = x + MACARON_WEIGHT * gate_mod * acc.reshape(bs, tt, d)


def _ffn(x, mod, gain, w_up, w_down, *, bs, tt, mix=None):
    s, t, d = x.shape
    ff = w_down.shape[0]
    n_fc = 2 if ff % (2 * LANES) == 0 else 1
    grid = (s // bs, t // tt)
    tile = lambda w: pl.BlockSpec((bs, tt, w), lambda i, j: (i, j, 0))
    mod_spec = pl.BlockSpec((bs, N_MOD, d), lambda i, j: (i, 0, 0))
    if mix is None:
        args = (x, mod, gain, w_up, w_down)
        specs = [tile(d), mod_spec, _resident(gain.shape), _resident(w_up.shape),
                 _resident(w_down.shape)]
    else:
        sb, rt, w_out = mix
        args = (x, sb, rt, mod, gain, w_out, w_up, w_down)
        specs = [tile(d), tile(SB_WIDTH), tile(RET_WIDTH), mod_spec, _resident(gain.shape),
                 _resident(w_out.shape), _resident(w_up.shape), _resident(w_down.shape)]
    return pl.pallas_call(
        functools.partial(_ffn_body, with_mix=mix is not None, n_fc=n_fc),
        grid=grid,
        in_specs=specs,
        out_specs=tile(d),
        out_shape=jax.ShapeDtypeStruct((s, t, d), F32),
        compiler_params=_cparams(("parallel", "parallel")),
        name="ffn_mix" if mix is not None else "ffn",
    )(*args)


def _proj_body(x_ref, mod_ref, gain_ref, win_ref, qg_ref, kg_ref, cos_ref, sin_ref, bd_ref,
               q_ref, kf_ref, ks_ref, vf_ref, vs_ref, rq_ref, rk_ref, rv_ref, rg_ref):
    bs, tt, d = x_ref.shape
    m = bs * tt
    mod = mod_ref[...]
    h = _ada_rms(x_ref[...], gain_ref[...], mod[:, 3:4, :], mod[:, 4:5, :])
    h = h.astype(BF16).reshape(m, d)

    def cols(a, w):
        return jnp.dot(h, win_ref[:, a:a + w], preferred_element_type=F32)

    qk = cols(0, 2 * SB_WIDTH)
    sq = (qk * qk).astype(BF16)
    bw = bd_ref.shape[0]
    ms = jnp.concatenate(
        [jnp.dot(sq[:, j * bw:(j + 1) * bw], bd_ref[...], preferred_element_type=F32)
         for j in range(2 * SB_WIDTH // bw)], axis=1) * (1.0 / SB_HEAD_DIM)
    qk = qk * lax.rsqrt(ms + NORM_EPS)
    qn = qk[:, :SB_WIDTH] * qg_ref[...]
    kn = qk[:, SB_WIDTH:] * kg_ref[...]
    q_ref[...] = (qn * (LOG2_E * SB_HEAD_DIM ** -0.5)).astype(BF16).reshape(bs, tt, SB_WIDTH)
    kf_ref[...] = kn.reshape(bs, tt, SB_WIDTH)
    v = cols(2 * SB_WIDTH, SB_WIDTH)
    vf_ref[...] = v.reshape(bs, tt, SB_WIDTH)
    first_head = lax.broadcasted_iota(jnp.int32, (m, SB_WIDTH), 1) % LANES < SB_HEAD_DIM
    for src, dst in ((kn, ks_ref), (v, vs_ref)):
        for half, keep in enumerate((first_head, jnp.logical_not(first_head))):
            dst[:, half] = jnp.where(keep, src, 0.0).astype(BF16).reshape(bs, tt, SB_WIDTH)

    cos = cos_ref[...][None]
    sin = sin_ref[...][None]

    def rope(a, scale):
        y = cols(a, RET_WIDTH)
        outs = []
        for hh in range(RET_HEADS):
            yh = y[:, hh * RET_HEAD_DIM:(hh + 1) * RET_HEAD_DIM]
            sw = pltpu.roll(yh, RET_HEAD_DIM // 2, axis=1)
            r = yh.reshape(bs, tt, RET_HEAD_DIM) * cos + sw.reshape(bs, tt, RET_HEAD_DIM) * sin
            outs.append(r)
        r = jnp.concatenate(outs, axis=-1)
        if scale != 1.0:
            r = r * scale
        return r.astype(BF16)

    base = 3 * SB_WIDTH
    rq_ref[...] = rope(base, 1.0)
    rk_ref[...] = rope(base + RET_WIDTH, RET_HEAD_DIM ** -0.5)
    rv_ref[...] = cols(base + 2 * RET_WIDTH, RET_WIDTH).astype(BF16).reshape(bs, tt, RET_WIDTH)
    rg_ref[...] = _silu(cols(base + 3 * RET_WIDTH, RET_WIDTH)).astype(BF16).reshape(bs, tt, RET_WIDTH)


def _mix_proj(x, mod, gain, w_in, qg, kg, cos, sin, bd, *, bs, tt):
    s, t, d = x.shape
    grid = (s // bs, t // tt)
    tile = lambda w: pl.BlockSpec((bs, tt, w), lambda i, j: (i, j, 0))
    tab = pl.BlockSpec((tt, RET_HEAD_DIM), lambda i, j: (j, 0))
    w = SB_WIDTH
    out = lambda dt: jax.ShapeDtypeStruct((s, t, w), dt)
    split = pl.BlockSpec((bs, 2, tt, w), lambda i, j: (i, 0, j, 0))
    out_split = jax.ShapeDtypeStruct((s, 2, t, w), BF16)
    return pl.pallas_call(
        _proj_body,
        grid=grid,
        in_specs=[tile(d), pl.BlockSpec((bs, N_MOD, d), lambda i, j: (i, 0, 0)),
                  _resident(gain.shape), _resident(w_in.shape), _resident(qg.shape),
                  _resident(kg.shape), tab, tab, _resident(bd.shape)],
        out_specs=[tile(w), tile(w), split, tile(w), split] + [tile(w)] * 4,
        out_shape=[out(BF16), out(F32), out_split, out(F32), out_split,
                   out(BF16), out(BF16), out(BF16), out(BF16)],
        compiler_params=_cparams(("parallel", "parallel")),
        name="mix_proj",
    )(x, mod, gain, w_in, qg, kg, cos, sin, bd)


def _split_heads(blk, axis):
    head_axis = 1 - axis
    pos = lax.broadcasted_iota(jnp.int32, blk.shape, head_axis)
    zero = jnp.zeros_like(blk)
    return jnp.concatenate([jnp.where(pos < SB_HEAD_DIM, blk, zero),
                            jnp.where(pos >= SB_HEAD_DIM, blk, zero)], axis=axis)


def _sb_sweep(qs, groups, tri_ones2, acc_ref, car_ref, *, rows, init):
    tk = KEY_BLOCK
    nt = (((1,), (1,)), ((), ()))
    n = len(qs) * HEADS_PER_TILE * rows
    zs = []
    for k_ops, _, transposed, _ in groups:
        for q, k_op in zip(qs, k_ops):
            if transposed:
                z = jnp.dot(q, k_op, preferred_element_type=F32)
            else:
                z = lax.dot_general(q, k_op, nt, preferred_element_type=F32)
            zs += [z[:, :tk], z[:, tk:]]
    z = jnp.concatenate(zs, axis=0)
    rest = jnp.maximum(jnp.log(1.0 + jnp.exp2(jnp.minimum(z, SOFTPLUS_CLAMP))) * LOG2_E, z)
    log_beta = z - rest
    row = lax.broadcasted_iota(jnp.int32, (n, tk), 0) % rows
    col = lax.broadcasted_iota(jnp.int32, (n, tk), 1)
    mask = col < row
    rest = jnp.concatenate(
        [jnp.where(mask, rest[g * n:(g + 1) * n], 0.0) if grp[3] else rest[g * n:(g + 1) * n]
         for g, grp in enumerate(groups)], axis=0)
    hi = rest.astype(BF16)
    lo = (rest - hi.astype(F32)).astype(BF16)
    cs = jnp.dot(jnp.concatenate([hi, lo], axis=1), tri_ones2, preferred_element_type=F32)
    run = None if init else car_ref[...]
    ws = []
    for g, grp in enumerate(groups):
        between = cs[g * n:(g + 1) * n, :tk]
        if run is not None:
            between = between + run
        w = jnp.exp2(log_beta[g * n:(g + 1) * n] + between)
        if grp[3]:
            w = jnp.where(mask, w, 0.0)
        ws.append(w.astype(BF16))
        sums = cs[g * n:(g + 1) * n, tk:]
        run = sums if run is None else run + sums
    car_ref[...] = run
    for p in range(len(qs)):
        pv = None
        for w, (_, v_ops, transposed, _) in zip(ws, groups):
            w2 = jnp.concatenate([w[(2 * p) * rows:(2 * p + 1) * rows],
                                  w[(2 * p + 1) * rows:(2 * p + 2) * rows]], axis=1)
            if transposed:
                d = lax.dot_general(w2, v_ops[p], nt, preferred_element_type=F32)
            else:
                d = jnp.dot(w2, v_ops[p], preferred_element_type=F32)
            pv = d if pv is None else pv + d
        if init:
            acc_ref[p * rows:(p + 1) * rows, :] = pv
        else:
            acc_ref[p * rows:(p + 1) * rows, :] += pv


def _sb_live(car_ref):
    m = car_ref[...]
    while m.shape[0] > 8:
        half = m.shape[0] // 2
        m = jnp.maximum(m[:half], m[half:])
    return (jnp.max(m) > F32_EXP2_ZERO).astype(jnp.int32)


def _sb_store(o_ref, acc_ref, rows, tiles):
    for p in range(tiles):
        o_ref[0, :, p * LANES:(p + 1) * LANES] = acc_ref[p * rows:(p + 1) * rows, :].astype(o_ref.dtype)


def _sb_prompt_body(q_ref, k_ref, v_ref, to_ref, o_ref, acc_ref, car_ref):
    qi = pl.program_id(1)
    rows = q_ref.shape[1]
    tk = KEY_BLOCK
    tiles = q_ref.shape[2] // LANES
    lanes = [slice(p * LANES, (p + 1) * LANES) for p in range(tiles)]
    tri_ones2 = to_ref[...]
    qs = [q_ref[0, :, sl] for sl in lanes]

    def block(kb, masked):
        keys = pl.ds(pl.multiple_of(kb * tk, tk), tk)
        ops = lambda ref: [jnp.concatenate([ref[0, 0, keys, sl], ref[0, 1, keys, sl]], axis=0)
                           for sl in lanes]
        return ops(k_ref), ops(v_ref), False, masked

    sweep = functools.partial(_sb_sweep, qs, tri_ones2=tri_ones2, acc_ref=acc_ref,
                              car_ref=car_ref, rows=rows)

    @pl.when(qi == 0)
    def _():
        sweep([block(qi, True)], init=True)

    @pl.when(qi > 0)
    def _():
        sweep([block(qi, True), block(qi - 1, False)], init=True)

    def body(c):
        sweep([block(c[0], False)], init=False)
        return c[0] - 1, _sb_live(car_ref)

    lax.while_loop(lambda c: jnp.logical_and(c[0] >= 0, c[1] > 0), body,
                   (qi - 2, _sb_live(car_ref)))
    _sb_store(o_ref, acc_ref, rows, tiles)


def _sb_decode_body(q_ref, kn_ref, vn_ref, kt_hbm, vt_hbm, to_ref, o_ref,
                    acc_ref, car_ref, kbuf, vbuf, sem):
    b = pl.program_id(0)
    nb = pl.num_programs(0)
    slot = b % 2
    rows = q_ref.shape[1]
    tk = KEY_BLOCK
    tiles = q_ref.shape[2] // LANES
    lanes = [slice(p * LANES, (p + 1) * LANES) for p in range(tiles)]
    chunk = kbuf.shape[3]
    subs = chunk // tk
    n_chunks = kt_hbm.shape[3] // chunk
    tri_ones2 = to_ref[...]
    qs = [q_ref[0, :, sl] for sl in lanes]

    def copies(stream, ci, to_slot):
        start = pl.multiple_of(ci * chunk, chunk)
        return (pltpu.make_async_copy(kt_hbm.at[stream, :, :, pl.ds(start, chunk)],
                                      kbuf.at[to_slot], sem.at[0, to_slot]),
                pltpu.make_async_copy(vt_hbm.at[stream, :, :, pl.ds(start, chunk)],
                                      vbuf.at[to_slot], sem.at[1, to_slot]))

    @pl.when(b == 0)
    def _():
        for cp in copies(b, n_chunks - 1, slot):
            cp.start()

    @pl.when(b + 1 < nb)
    def _():
        for cp in copies(b + 1, n_chunks - 1, 1 - slot):
            cp.start()

    for cp in copies(b, n_chunks - 1, slot):
        cp.wait()

    def cached(sub):
        def ops(buf):
            return [_split_heads(
                buf[slot, pl.ds(p * HEADS_PER_TILE, HEADS_PER_TILE), :, pl.ds(sub * tk, tk)]
                .reshape(LANES, tk).astype(BF16), 1) for p in range(tiles)]
        return ops(kbuf), ops(vbuf), True, False

    def new_keys():
        pad = jnp.zeros((tk - rows, LANES), BF16)
        ops = lambda ref: [jnp.concatenate([ref[0, 0, :, sl], pad, ref[0, 1, :, sl], pad], axis=0)
                           for sl in lanes]
        return ops(kn_ref), ops(vn_ref), False, True

    sweep = functools.partial(_sb_sweep, qs, tri_ones2=tri_ones2, acc_ref=acc_ref,
                              car_ref=car_ref, rows=rows)

    def older(first_sub):
        for sub in reversed(range(first_sub + 1)):
            pl.when(_sb_live(car_ref) > 0)(lambda sub=sub: sweep([cached(sub)], init=False))

    sweep([new_keys(), cached(subs - 1)], init=True)
    older(subs - 2)

    def body(c):
        for cp in copies(b, c[0], slot):
            cp.start()
        for cp in copies(b, c[0], slot):
            cp.wait()
        older(subs - 1)
        return c[0] - 1, _sb_live(car_ref)

    lax.while_loop(lambda c: jnp.logical_and(c[0] >= 0, c[1] > 0), body,
                   (jnp.int32(n_chunks - 2), _sb_live(car_ref)))
    _sb_store(o_ref, acc_ref, rows, tiles)


def _tri_ones2():
    j = lax.broadcasted_iota(jnp.int32, (2 * KEY_BLOCK, 2 * KEY_BLOCK), 0) % KEY_BLOCK
    s = lax.broadcasted_iota(jnp.int32, (2 * KEY_BLOCK, 2 * KEY_BLOCK), 1)
    return -jnp.logical_or(j > s, s >= KEY_BLOCK).astype(BF16)


def _sb_scratch(rows, width):
    tiles = width // LANES
    return [pltpu.VMEM((tiles * rows, LANES), F32),
            pltpu.VMEM((tiles * HEADS_PER_TILE * rows, LANES), F32)]


def _stick_break_prompt(q, k, v):
    b, t, w = q.shape
    rows = KEY_BLOCK
    qspec = pl.BlockSpec((1, rows, w), lambda i, j: (i, j, 0))
    full = pl.BlockSpec((1, 2, t, w), lambda i, j: (i, 0, 0, 0), pipeline_mode=pl.Buffered(1))
    to = _tri_ones2()
    return pl.pallas_call(
        _sb_prompt_body,
        grid=(b, t // rows),
        in_specs=[qspec, full, full, _resident(to.shape)],
        out_specs=qspec,
        out_shape=jax.ShapeDtypeStruct((b, t, w), BF16),
        scratch_shapes=_sb_scratch(rows, w),
        compiler_params=_cparams(("parallel", "arbitrary")),
        name="stick_break_prompt",
    )(q, k, v, to)


def _stick_break_decode(q, k_new, v_new, kt_past, vt_past):
    b, rows, w = q.shape
    past = kt_past.shape[3]
    chunk = min(DECODE_CHUNK, past)
    assert past % chunk == 0 and chunk % KEY_BLOCK == 0 and rows <= KEY_BLOCK
    new = pl.BlockSpec((1, rows, w), lambda i: (i, 0, 0))
    new_split = pl.BlockSpec((1, 2, rows, w), lambda i: (i, 0, 0, 0))
    hbm = pl.BlockSpec(memory_space=pl.ANY)
    to = _tri_ones2()
    return pl.pallas_call(
        _sb_decode_body,
        grid=(b,),
        in_specs=[new, new_split, new_split, hbm, hbm, _resident(to.shape)],
        out_specs=new,
        out_shape=jax.ShapeDtypeStruct((b, rows, w), BF16),
        scratch_shapes=_sb_scratch(rows, w) + [
            pltpu.VMEM((2, SB_HEADS, SB_HEAD_DIM, chunk), F32),
            pltpu.VMEM((2, SB_HEADS, SB_HEAD_DIM, chunk), F32),
            pltpu.SemaphoreType.DMA((2, 2))],
        compiler_params=_cparams(("arbitrary",)),
        name="stick_break_decode",
    )(q, k_new, v_new, kt_past, vt_past, to)


def _ret_body(q_ref, k_ref, v_ref, g_ref, s0_ref, dm_ref, qd_ref, kd_ref, cd_ref,
              o_ref, so_ref, s_scr):
    n = pl.program_id(1)

    @pl.when(n == 0)
    def _():
        s_scr[...] = s0_ref[0]

    for h in range(RET_HEADS):
        sl = slice(h * RET_HEAD_DIM, (h + 1) * RET_HEAD_DIM)
        q = q_ref[0, :, sl]
        k = k_ref[0, :, sl]
        v = v_ref[0, :, sl]
        state = s_scr[h]
        scores = lax.dot_general(q, k, (((1,), (1,)), ((), ())),
                                 preferred_element_type=F32) * dm_ref[h]
        intra = jnp.dot(scores.astype(BF16), v, preferred_element_type=F32)
        q_in = (q.astype(F32) * qd_ref[h]).astype(BF16)
        cross = jnp.dot(q_in, state.astype(BF16), preferred_element_type=F32)
        k_out = (k.astype(F32) * kd_ref[h]).astype(BF16)
        kv = lax.dot_general(k_out, v, (((0,), (0,)), ((), ())), preferred_element_type=F32)
        s_scr[h] = cd_ref[h] * state + kv
        o = intra + cross
        oc = o - jnp.mean(o, axis=-1, keepdims=True)
        gn = oc * lax.rsqrt(jnp.mean(oc * oc, axis=-1, keepdims=True) + NORM_EPS)
        o_ref[0, :, sl] = (g_ref[0, :, sl].astype(F32) * gn).astype(o_ref.dtype)

    @pl.when(n == pl.num_programs(1) - 1)
    def _():
        so_ref[0] = s_scr[...]


def _retention_tables(c):
    lg = jnp.log1p(-jnp.exp2(-5.0 - jnp.arange(RET_HEADS, dtype=F32)))
    idx = jnp.arange(c, dtype=F32)
    diff = idx[:, None] - idx[None, :]
    dm = jnp.where(diff[None] >= 0, jnp.exp(jnp.maximum(diff, 0.0)[None] * lg[:, None, None]), 0.0)
    lanes = jnp.ones((1, 1, RET_HEAD_DIM), F32)
    qd = jnp.exp((idx + 1.0)[None, :, None] * lg[:, None, None]) * lanes
    kd = jnp.exp((c - 1.0 - idx)[None, :, None] * lg[:, None, None]) * lanes
    cd = jnp.exp(c * lg)[:, None, None] * lanes
    return dm, qd, kd, cd


def _retention(rq, rk, rv, rg, s0, chunk):
    b, t, w = rq.shape
    c = min(chunk, t)
    dm, qd, kd, cd = _retention_tables(c)
    tile = pl.BlockSpec((1, c, w), lambda i, n: (i, n, 0))
    st = pl.BlockSpec((1, RET_HEADS, RET_HEAD_DIM, RET_HEAD_DIM), lambda i, n: (i, 0, 0, 0))
    const = lambda a: pl.BlockSpec(a.shape, lambda i, n: (0,) * a.ndim)
    return pl.pallas_call(
        _ret_body,
        grid=(b, t // c),
        in_specs=[tile, tile, tile, tile, st, const(dm), const(qd), const(kd), const(cd)],
        out_specs=[tile, st],
        out_shape=[jax.ShapeDtypeStruct((b, t, w), BF16),
                   jax.ShapeDtypeStruct(s0.shape, F32)],
        scratch_shapes=[pltpu.VMEM((RET_HEADS, RET_HEAD_DIM, RET_HEAD_DIM), F32)],
        compiler_params=_cparams(("parallel", "arbitrary")),
        name="retention",
    )(rq, rk, rv, rg, s0, dm, qd, kd, cd)


def _rope_tables(pos):
    half = RET_HEAD_DIM // 2
    inv_freq = ROPE_BASE ** (-jnp.arange(half, dtype=F32) / half)
    ang = pos.astype(F32)[:, None] * inv_freq[None, :]
    cos, sin = jnp.cos(ang), jnp.sin(ang)
    return jnp.concatenate([cos, cos], axis=-1), jnp.concatenate([-sin, sin], axis=-1)


def _block_diag_ones(width):
    i = lax.broadcasted_iota(jnp.int32, (width, width), 0) // SB_HEAD_DIM
    j = lax.broadcasted_iota(jnp.int32, (width, width), 1) // SB_HEAD_DIM
    return (i == j).astype(BF16)


def _layer(x, mod, pos, past, w, *, bs, tt):
    s, t, d = x.shape
    x = _ffn(x, mod, w["norm_ffn1"], w["ffn1_w_up"], w["ffn1_w_down"], bs=bs, tt=tt)
    cos, sin = _rope_tables(pos)
    q, k_f, k_b, v_f, v_b, rq, rk, rv, rg = _mix_proj(
        x, mod, w["norm_mix"], w["w_in"], w["sb_q_gain"], w["sb_k_gain"], cos, sin,
        _block_diag_ones(2 * LANES), bs=bs, tt=tt)
    if past is None:
        sb_o = _stick_break_prompt(q, k_b, v_b)
        s0 = jnp.zeros((s, RET_HEADS, RET_HEAD_DIM, RET_HEAD_DIM), F32)
        chunk = PROMPT_RET_CHUNK
    else:
        k_past, v_past, s0 = past
        sb_o = _stick_break_decode(q, k_b, v_b, jnp.transpose(k_past, (0, 2, 3, 1)),
                                   jnp.transpose(v_past, (0, 2, 3, 1)))
        chunk = t
    ret_o, state = _retention(rq, rk, rv, rg, s0, chunk)
    x = _ffn(x, mod, w["norm_ffn2"], w["ffn2_w_up"], w["ffn2_w_down"], bs=bs, tt=tt,
             mix=(sb_o, ret_o, w["w_out"]))
    heads = (s, t, SB_HEADS, SB_HEAD_DIM)
    return x, k_f.reshape(heads), v_f.reshape(heads), state


def kernel(x_prompt, x_sample, cache_sb_k, cache_sb_v, state_ret, c_prompt, c_sample, w_ada, b_ada, norm_ffn1, norm_mix, norm_ffn2, ffn1_w_up, ffn1_w_down, w_in, sb_q_gain, sb_k_gain, w_out, ffn2_w_up, ffn2_w_down):
    depth = w_ada.shape[0]
    bp, tp, d = x_prompt.shape
    bd, td, _ = x_sample.shape
    past_len = cache_sb_k.shape[2]
    pos_p = jnp.arange(tp, dtype=jnp.int32)
    pos_s = past_len + jnp.arange(td, dtype=jnp.int32)
    c_all = jnp.concatenate([c_prompt, c_sample], axis=0)
    tt_p = min(512, tp)
    bs_s = max(1, min(bd, 512 // td))
    assert tp % tt_p == 0 and bd % bs_s == 0

    y_p, y_s = x_prompt, x_sample
    outs = [[] for _ in range(6)]
    for l in range(depth):
        w = {
            "norm_ffn1": norm_ffn1[l][None], "norm_mix": norm_mix[l][None],
            "norm_ffn2": norm_ffn2[l][None],
            "ffn1_w_up": ffn1_w_up[l].astype(BF16), "ffn1_w_down": ffn1_w_down[l].astype(BF16),
            "ffn2_w_up": ffn2_w_up[l].astype(BF16), "ffn2_w_down": ffn2_w_down[l].astype(BF16),
            "w_in": w_in[l].astype(BF16), "w_out": w_out[l].astype(BF16),
            "sb_q_gain": jnp.tile(sb_q_gain[l], SB_HEADS)[None],
            "sb_k_gain": jnp.tile(sb_k_gain[l], SB_HEADS)[None],
        }
        mod = _modulation(c_all, w_ada[l], b_ada[l]).reshape(bp + bd, N_MOD, d)
        y_p, kp, vp, sp = _layer(y_p, mod[:bp], pos_p, None, w, bs=1, tt=tt_p)
        y_s, ks, vs, ss = _layer(y_s, mod[bp:], pos_s,
                                 (cache_sb_k[l], cache_sb_v[l], state_ret[l]), w,
                                 bs=bs_s, tt=td)
        for lst, a in zip(outs, (kp, vp, sp, ks, vs, ss)):
            lst.append(a)
    return (y_p, y_s) + tuple(jnp.stack(a) for a in outs)
```

```python
import functools

import jax
import jax.numpy as jnp
from jax import lax
from jax.experimental import pallas as pl
from jax.experimental.pallas import tpu as pltpu

F32 = jnp.float32
BF16 = jnp.bfloat16

N_MOD = 9
NORM_EPS = 1e-6
MACARON_WEIGHT = 0.5
ROPE_BASE = 10000.0
SB_HEADS = 8
SB_HEAD_DIM = 64
SB_WIDTH = SB_HEADS * SB_HEAD_DIM
RET_HEADS = 4
RET_HEAD_DIM = 128
RET_WIDTH = RET_HEADS * RET_HEAD_DIM

LANES = 128
KEY_BLOCK = LANES
HEADS_PER_TILE = LANES // SB_HEAD_DIM
LOG2_E = 1.4426950408889634
F32_EXP2_ZERO = -126.0
SOFTPLUS_CLAMP = 64.0
FIRST_SWEEP_BLOCKS = 3
PROMPT_RET_CHUNK = 256
DECODE_CHUNK = 256
VMEM_LIMIT = 56 * 1024 * 1024


def _cparams(sem):
    return pltpu.CompilerParams(dimension_semantics=sem, vmem_limit_bytes=VMEM_LIMIT)


def _resident(shape):
    nd = len(shape)
    return pl.BlockSpec(shape, lambda *_: (0,) * nd, pipeline_mode=pl.Buffered(1))


def _silu(x):
    return x * (1.0 / (1.0 + jnp.exp(-x)))


def _mod_body(c_ref, w_ref, b_ref, o_ref):
    c = c_ref[...]
    o_ref[...] = jnp.dot(_silu(c).astype(BF16), w_ref[...].astype(BF16),
                         preferred_element_type=F32) + b_ref[...]


def _modulation(c, w_ada, b_ada):
    s, d = c.shape
    n = w_ada.shape[1]
    tn = d
    return pl.pallas_call(
        _mod_body,
        grid=(n // tn,),
        in_specs=[pl.BlockSpec((s, d), lambda j: (0, 0)),
                  pl.BlockSpec((d, tn), lambda j: (0, j)),
                  pl.BlockSpec((1, tn), lambda j: (0, j))],
        out_specs=pl.BlockSpec((s, tn), lambda j: (0, j)),
        out_shape=jax.ShapeDtypeStruct((s, n), F32),
        compiler_params=_cparams(("arbitrary",)),
        name="modulation",
    )(c, w_ada, b_ada.reshape(1, n))


def _ada_rms(x, gain, shift, scale):
    inv = lax.rsqrt(jnp.mean(x * x, axis=-1, keepdims=True) + NORM_EPS)
    return (x * inv * gain) * (1.0 + scale) + shift


def _ffn_body(*refs, with_mix, n_fc):
    if with_mix:
        x_ref, sb_ref, rt_ref, mod_ref, gain_ref, wout_ref, wup_ref, wdn_ref, o_ref = refs
    else:
        x_ref, mod_ref, gain_ref, wup_ref, wdn_ref, o_ref = refs
    bs, tt, d = x_ref.shape
    ff = wdn_ref.shape[0]
    fc = ff // n_fc
    x = x_ref[...]
    mod = mod_ref[...]
    if with_mix:
        sb = sb_ref[...].reshape(bs * tt, SB_WIDTH)
        rt = rt_ref[...].reshape(bs * tt, RET_WIDTH)
        mixed = (jnp.dot(sb, wout_ref[:SB_WIDTH, :], preferred_element_type=F32)
                 + jnp.dot(rt, wout_ref[SB_WIDTH:, :], preferred_element_type=F32))
        x = x + mod[:, 5:6, :] * mixed.reshape(bs, tt, d)
        shift, scale, gate_mod = mod[:, 6:7, :], mod[:, 7:8, :], mod[:, 8:9, :]
    else:
        shift, scale, gate_mod = mod[:, 0:1, :], mod[:, 1:2, :], mod[:, 2:3, :]
    h = _ada_rms(x, gain_ref[...], shift, scale).astype(BF16).reshape(bs * tt, d)
    acc = jnp.zeros((bs * tt, d), F32)
    for c in range(n_fc):
        gate = jnp.dot(h, wup_ref[:, c * fc:(c + 1) * fc], preferred_element_type=F32)
        up = jnp.dot(h, wup_ref[:, ff + c * fc:ff + (c + 1) * fc], preferred_element_type=F32)
        act = (_silu(gate) * up).astype(BF16)
        acc = acc + jnp.dot(act, wdn_ref[c * fc:(c + 1) * fc, :], preferred_element_type=F32)
    o_ref[...] = x + MACARON_WEIGHT * gate_mod * acc.reshape(bs, tt, d)


def _ffn(x, mod, gain, w_up, w_down, *, bs, tt, mix=None):
    s, t, d = x.shape
    ff = w_down.shape[0]
    n_fc = 2 if ff % (2 * LANES) == 0 else 1
    grid = (s // bs, t // tt)
    tile = lambda w: pl.BlockSpec((bs, tt, w), lambda i, j: (i, j, 0))
    mod_spec = pl.BlockSpec((bs, N_MOD, d), lambda i, j: (i, 0, 0))
    if mix is None:
        args = (x, mod, gain, w_up, w_down)
        specs = [tile(d), mod_spec, _resident(gain.shape), _resident(w_up.shape),
                 _resident(w_down.shape)]
    else:
        sb, rt, w_out = mix
        args = (x, sb, rt, mod, gain, w_out, w_up, w_down)
        specs = [tile(d), tile(SB_WIDTH), tile(RET_WIDTH), mod_spec, _resident(gain.shape),
                 _resident(w_out.shape), _resident(w_up.shape), _resident(w_down.shape)]
    return pl.pallas_call(
        functools.partial(_ffn_body, with_mix=mix is not None, n_fc=n_fc),
        grid=grid,
        in_specs=specs,
        out_specs=tile(d),
        out_shape=jax.ShapeDtypeStruct((s, t, d), F32),
        compiler_params=_cparams(("parallel", "parallel")),
        name="ffn_mix" if mix is not None else "ffn",
    )(*args)


def _proj_body(x_ref, mod_ref, gain_ref, win_ref, qg_ref, kg_ref, cos_ref, sin_ref, bd_ref,
               q_ref, kf_ref, ks_ref, vf_ref, vs_ref, rq_ref, rk_ref, rv_ref, rg_ref):
    bs, tt, d = x_ref.shape
    m = bs * tt
    mod = mod_ref[...]
    h = _ada_rms(x_ref[...], gain_ref[...], mod[:, 3:4, :], mod[:, 4:5, :])
    h = h.astype(BF16).reshape(m, d)

    def cols(a, w):
        return jnp.dot(h, win_ref[:, a:a + w], preferred_element_type=F32)

    qk = cols(0, 2 * SB_WIDTH)
    sq = (qk * qk).astype(BF16)
    bw = bd_ref.shape[0]
    ms = jnp.concatenate(
        [jnp.dot(sq[:, j * bw:(j + 1) * bw], bd_ref[...], preferred_element_type=F32)
         for j in range(2 * SB_WIDTH // bw)], axis=1) * (1.0 / SB_HEAD_DIM)
    qk = qk * lax.rsqrt(ms + NORM_EPS)
    qn = qk[:, :SB_WIDTH] * qg_ref[...]
    kn = qk[:, SB_WIDTH:] * kg_ref[...]
    q_ref[...] = (qn * (LOG2_E * SB_HEAD_DIM ** -0.5)).astype(BF16).reshape(bs, tt, SB_WIDTH)
    kf_ref[...] = kn.reshape(bs, tt, SB_WIDTH)
    v = cols(2 * SB_WIDTH, SB_WIDTH)
    vf_ref[...] = v.reshape(bs, tt, SB_WIDTH)
    first_head = lax.broadcasted_iota(jnp.int32, (m, SB_WIDTH), 1) % LANES < SB_HEAD_DIM
    for src, dst in ((kn, ks_ref), (v, vs_ref)):
        for half, keep in enumerate((first_head, jnp.logical_not(first_head))):
            dst[:, half] = jnp.where(keep, src, 0.0).astype(BF16).reshape(bs, tt, SB_WIDTH)

    cos = cos_ref[...][None]
    sin = sin_ref[...][None]

    def rope(a, scale):
        y = cols(a, RET_WIDTH)
        outs = []
        for hh in range(RET_HEADS):
            yh = y[:, hh * RET_HEAD_DIM:(hh + 1) * RET_HEAD_DIM]
            sw = pltpu.roll(yh, RET_HEAD_DIM // 2, axis=1)
            r = yh.reshape(bs, tt, RET_HEAD_DIM) * cos + sw.reshape(bs, tt, RET_HEAD_DIM) * sin
            outs.append(r)
        r = jnp.concatenate(outs, axis=-1)
        if scale != 1.0:
            r = r * scale
        return r.astype(BF16)

    base = 3 * SB_WIDTH
    rq_ref[...] = rope(base, 1.0)
    rk_ref[...] = rope(base + RET_WIDTH, RET_HEAD_DIM ** -0.5)
    rv_ref[...] = cols(base + 2 * RET_WIDTH, RET_WIDTH).astype(BF16).reshape(bs, tt, RET_WIDTH)
    rg_ref[...] = _silu(cols(base + 3 * RET_WIDTH, RET_WIDTH)).astype(BF16).reshape(bs, tt, RET_WIDTH)


def _mix_proj(x, mod, gain, w_in, qg, kg, cos, sin, bd, *, bs, tt):
    s, t, d = x.shape
    grid = (s // bs, t // tt)
    tile = lambda w: pl.BlockSpec((bs, tt, w), lambda i, j: (i, j, 0))
    tab = pl.BlockSpec((tt, RET_HEAD_DIM), lambda i, j: (j, 0))
    w = SB_WIDTH
    out = lambda dt: jax.ShapeDtypeStruct((s, t, w), dt)
    split = pl.BlockSpec((bs, 2, tt, w), lambda i, j: (i, 0, j, 0))
    out_split = jax.ShapeDtypeStruct((s, 2, t, w), BF16)
    return pl.pallas_call(
        _proj_body,
        grid=grid,
        in_specs=[tile(d), pl.BlockSpec((bs, N_MOD, d), lambda i, j: (i, 0, 0)),
                  _resident(gain.shape), _resident(w_in.shape), _resident(qg.shape),
                  _resident(kg.shape), tab, tab, _resident(bd.shape)],
        out_specs=[tile(w), tile(w), split, tile(w), split] + [tile(w)] * 4,
        out_shape=[out(BF16), out(F32), out_split, out(F32), out_split,
                   out(BF16), out(BF16), out(BF16), out(BF16)],
        compiler_params=_cparams(("parallel", "parallel")),
        name="mix_proj",
    )(x, mod, gain, w_in, qg, kg, cos, sin, bd)


def _split_heads(blk, axis):
    head_axis = 1 - axis
    pos = lax.broadcasted_iota(jnp.int32, blk.shape, head_axis)
    zero = jnp.zeros_like(blk)
    return jnp.concatenate([jnp.where(pos < SB_HEAD_DIM, blk, zero),
                            jnp.where(pos >= SB_HEAD_DIM, blk, zero)], axis=axis)


def _sb_sweep(qs, groups, tri_ones2, acc_ref, car_ref, *, rows, init):
    tk = KEY_BLOCK
    nt = (((1,), (1,)), ((), ()))
    n = len(qs) * HEADS_PER_TILE * rows
    zs = []
    for k_ops, _, transposed, _ in groups:
        for q, k_op in zip(qs, k_ops):
            if transposed:
                z = jnp.dot(q, k_op, preferred_element_type=F32)
            else:
                z = lax.dot_general(q, k_op, nt, preferred_element_type=F32)
            zs += [z[:, :tk], z[:, tk:]]
    z = jnp.concatenate(zs, axis=0)
    rest = jnp.maximum(jnp.log(1.0 + jnp.exp2(jnp.minimum(z, SOFTPLUS_CLAMP))) * LOG2_E, z)
    log_beta = z - rest
    row = lax.broadcasted_iota(jnp.int32, (n, tk), 0) % rows
    col = lax.broadcasted_iota(jnp.int32, (n, tk), 1)
    mask = col < row
    rest = jnp.concatenate(
        [jnp.where(mask, rest[g * n:(g + 1) * n], 0.0) if grp[3] else rest[g * n:(g + 1) * n]
         for g, grp in enumerate(groups)], axis=0)
    hi = rest.astype(BF16)
    lo = (rest - hi.astype(F32)).astype(BF16)
    cs = jnp.dot(jnp.concatenate([hi, lo], axis=1), tri_ones2, preferred_element_type=F32)
    run = None if init else car_ref[...]
    ws = []
    for g, grp in enumerate(groups):
        between = cs[g * n:(g + 1) * n, :tk]
        if run is not None:
            between = between + run
        w = jnp.exp2(log_beta[g * n:(g + 1) * n] + between)
        if grp[3]:
            w = jnp.where(mask, w, 0.0)
        ws.append(w.astype(BF16))
        sums = cs[g * n:(g + 1) * n, tk:]
        run = sums if run is None else run + sums
    car_ref[...] = run
    for p in range(len(qs)):
        pv = None
        for w, (_, v_ops, transposed, _) in zip(ws, groups):
            w2 = jnp.concatenate([w[(2 * p) * rows:(2 * p + 1) * rows],
                                  w[(2 * p + 1) * rows:(2 * p + 2) * rows]], axis=1)
            if transposed:
                d = lax.dot_general(w2, v_ops[p], nt, preferred_element_type=F32)
            else:
                d = jnp.dot(w2, v_ops[p], preferred_element_type=F32)
            pv = d if pv is None else pv + d
        if init:
            acc_ref[p * rows:(p + 1) * rows, :] = pv
        else:
            acc_ref[p * rows:(p + 1) * rows, :] += pv


def _sb_live(car_ref):
    m = car_ref[...]
    while m.shape[0] > 8:
        half = m.shape[0] // 2
        m = jnp.maximum(m[:half], m[half:])
    return (jnp.max(m) > F32_EXP2_ZERO).astype(jnp.int32)


def _sb_store(o_ref, acc_ref, rows, tiles):
    for p in range(tiles):
        o_ref[0, :, p * LANES:(p + 1) * LANES] = acc_ref[p * rows:(p + 1) * rows, :].astype(o_ref.dtype)


def _sb_prompt_body(q_ref, k_ref, v_ref, to_ref, o_ref, acc_ref, car_ref):
    qi = pl.program_id(1)
    rows = q_ref.shape[1]
    tk = KEY_BLOCK
    tiles = q_ref.shape[2] // LANES
    lanes = [slice(p * LANES, (p + 1) * LANES) for p in range(tiles)]
    tri_ones2 = to_ref[...]
    qs = [q_ref[0, :, sl] for sl in lanes]

    def block(kb, masked):
        keys = pl.ds(pl.multiple_of(kb * tk, tk), tk)
        ops = lambda ref: [jnp.concatenate([ref[0, 0, keys, sl], ref[0, 1, keys, sl]], axis=0)
                           for sl in lanes]
        return ops(k_ref), ops(v_ref), False, masked

    sweep = functools.partial(_sb_sweep, qs, tri_ones2=tri_ones2, acc_ref=acc_ref,
                              car_ref=car_ref, rows=rows)

    @pl.when(qi == 0)
    def _():
        sweep([block(qi, True)], init=True)

    @pl.when(qi == 1)
    def _():
        sweep([block(qi, True), block(qi - 1, False)], init=True)

    @pl.when(qi > 1)
    def _():
        sweep([block(qi, True), block(qi - 1, False), block(qi - 2, False)], init=True)

    def body(c):
        sweep([block(c[0], False)], init=False)
        return c[0] - 1, _sb_live(car_ref)

    lax.while_loop(lambda c: jnp.logical_and(c[0] >= 0, c[1] > 0), body,
                   (qi - FIRST_SWEEP_BLOCKS, _sb_live(car_ref)))
    _sb_store(o_ref, acc_ref, rows, tiles)


def _sb_decode_body(q_ref, kn_ref, vn_ref, kt_hbm, vt_hbm, to_ref, o_ref,
                    acc_ref, car_ref, kbuf, vbuf, sem):
    b = pl.program_id(0)
    nb = pl.num_programs(0)
    slot = b % 2
    rows = q_ref.shape[1]
    tk = KEY_BLOCK
    tiles = q_ref.shape[2] // LANES
    lanes = [slice(p * LANES, (p + 1) * LANES) for p in range(tiles)]
    chunk = kbuf.shape[3]
    subs = chunk // tk
    n_chunks = kt_hbm.shape[3] // chunk
    tri_ones2 = to_ref[...]
    qs = [q_ref[0, :, sl] for sl in lanes]

    def copies(stream, ci, to_slot):
        start = pl.multiple_of(ci * chunk, chunk)
        return (pltpu.make_async_copy(kt_hbm.at[stream, :, :, pl.ds(start, chunk)],
                                      kbuf.at[to_slot], sem.at[0, to_slot]),
                pltpu.make_async_copy(vt_hbm.at[stream, :, :, pl.ds(start, chunk)],
                                      vbuf.at[to_slot], sem.at[1, to_slot]))

    @pl.when(b == 0)
    def _():
        for cp in copies(b, n_chunks - 1, slot):
            cp.start()

    @pl.when(b + 1 < nb)
    def _():
        for cp in copies(b + 1, n_chunks - 1, 1 - slot):
            cp.start()

    for cp in copies(b, n_chunks - 1, slot):
        cp.wait()

    def cached(sub):
        def ops(buf):
            return [_split_heads(
                buf[slot, pl.ds(p * HEADS_PER_TILE, HEADS_PER_TILE), :, pl.ds(sub * tk, tk)]
                .reshape(LANES, tk).astype(BF16), 1) for p in range(tiles)]
        return ops(kbuf), ops(vbuf), True, False

    def new_keys():
        pad = jnp.zeros((tk - rows, LANES), BF16)
        ops = lambda ref: [jnp.concatenate([ref[0, 0, :, sl], pad, ref[0, 1, :, sl], pad], axis=0)
                           for sl in lanes]
        return ops(kn_ref), ops(vn_ref), False, True

    sweep = functools.partial(_sb_sweep, qs, tri_ones2=tri_ones2, acc_ref=acc_ref,
                              car_ref=car_ref, rows=rows)

    def older(first_sub):
        for sub in reversed(range(first_sub + 1)):
            pl.when(_sb_live(car_ref) > 0)(lambda sub=sub: sweep([cached(sub)], init=False))

    sweep([new_keys(), cached(subs - 1)], init=True)
    older(subs - 2)

    def body(c):
        for cp in copies(b, c[0], slot):
            cp.start()
        for cp in copies(b, c[0], slot):
            cp.wait()
        older(subs - 1)
        return c[0] - 1, _sb_live(car_ref)

    lax.while_loop(lambda c: jnp.logical_and(c[0] >= 0, c[1] > 0), body,
                   (jnp.int32(n_chunks - 2), _sb_live(car_ref)))
    _sb_store(o_ref, acc_ref, rows, tiles)


def _tri_ones2():
    j = lax.broadcasted_iota(jnp.int32, (2 * KEY_BLOCK, 2 * KEY_BLOCK), 0) % KEY_BLOCK
    s = lax.broadcasted_iota(jnp.int32, (2 * KEY_BLOCK, 2 * KEY_BLOCK), 1)
    return -jnp.logical_or(j > s, s >= KEY_BLOCK).astype(BF16)


def _sb_scratch(rows, width):
    tiles = width // LANES
    return [pltpu.VMEM((tiles * rows, LANES), F32),
            pltpu.VMEM((tiles * HEADS_PER_TILE * rows, LANES), F32)]


def _stick_break_prompt(q, k, v):
    b, t, w = q.shape
    rows = KEY_BLOCK
    qspec = pl.BlockSpec((1, rows, w), lambda i, j: (i, j, 0))
    full = pl.BlockSpec((1, 2, t, w), lambda i, j: (i, 0, 0, 0), pipeline_mode=pl.Buffered(1))
    to = _tri_ones2()
    return pl.pallas_call(
        _sb_prompt_body,
        grid=(b, t // rows),
        in_specs=[qspec, full, full, _resident(to.shape)],
        out_specs=qspec,
        out_shape=jax.ShapeDtypeStruct((b, t, w), BF16),
        scratch_shapes=_sb_scratch(rows, w),
        compiler_params=_cparams(("parallel", "arbitrary")),
        name="stick_break_prompt",
    )(q, k, v, to)


def _stick_break_decode(q, k_new, v_new, kt_past, vt_past):
    b, rows, w = q.shape
    past = kt_past.shape[3]
    chunk = min(DECODE_CHUNK, past)
    assert past % chunk == 0 and chunk % KEY_BLOCK == 0 and rows <= KEY_BLOCK
    new = pl.BlockSpec((1, rows, w), lambda i: (i, 0, 0))
    new_split = pl.BlockSpec((1, 2, rows, w), lambda i: (i, 0, 0, 0))
    hbm = pl.BlockSpec(memory_space=pl.ANY)
    to = _tri_ones2()
    return pl.pallas_call(
        _sb_decode_body,
        grid=(b,),
        in_specs=[new, new_split, new_split, hbm, hbm, _resident(to.shape)],
        out_specs=new,
        out_shape=jax.ShapeDtypeStruct((b, rows, w), BF16),
        scratch_shapes=_sb_scratch(rows, w) + [
            pltpu.VMEM((2, SB_HEADS, SB_HEAD_DIM, chunk), F32),
            pltpu.VMEM((2, SB_HEADS, SB_HEAD_DIM, chunk), F32),
            pltpu.SemaphoreType.DMA((2, 2))],
        compiler_params=_cparams(("arbitrary",)),
        name="stick_break_decode",
    )(q, k_new, v_new, kt_past, vt_past, to)


def _ret_body(q_ref, k_ref, v_ref, g_ref, s0_ref, dm_ref, qd_ref, kd_ref, cd_ref,
              o_ref, so_ref, s_scr):
    n = pl.program_id(1)

    @pl.when(n == 0)
    def _():
        s_scr[...] = s0_ref[0]

    for h in range(RET_HEADS):
        sl = slice(h * RET_HEAD_DIM, (h + 1) * RET_HEAD_DIM)
        q = q_ref[0, :, sl]
        k = k_ref[0, :, sl]
        v = v_ref[0, :, sl]
        state = s_scr[h]
        scores = lax.dot_general(q, k, (((1,), (1,)), ((), ())),
                                 preferred_element_type=F32) * dm_ref[h]
        intra = jnp.dot(scores.astype(BF16), v, preferred_element_type=F32)
        q_in = (q.astype(F32) * qd_ref[h]).astype(BF16)
        cross = jnp.dot(q_in, state.astype(BF16), preferred_element_type=F32)
        k_out = (k.astype(F32) * kd_ref[h]).astype(BF16)
        kv = lax.dot_general(k_out, v, (((0,), (0,)), ((), ())), preferred_element_type=F32)
        s_scr[h] = cd_ref[h] * state + kv
        o = intra + cross
        oc = o - jnp.mean(o, axis=-1, keepdims=True)
        gn = oc * lax.rsqrt(jnp.mean(oc * oc, axis=-1, keepdims=True) + NORM_EPS)
        o_ref[0, :, sl] = (g_ref[0, :, sl].astype(F32) * gn).astype(o_ref.dtype)

    @pl.when(n == pl.num_programs(1) - 1)
    def _():
        so_ref[0] = s_scr[...]


def _retention_tables(c):
    lg = jnp.log1p(-jnp.exp2(-5.0 - jnp.arange(RET_HEADS, dtype=F32)))
    idx = jnp.arange(c, dtype=F32)
    diff = idx[:, None] - idx[None, :]
    dm = jnp.where(diff[None] >= 0, jnp.exp(jnp.maximum(diff, 0.0)[None] * lg[:, None, None]), 0.0)
    lanes = jnp.ones((1, 1, RET_HEAD_DIM), F32)
    qd = jnp.exp((idx + 1.0)[None, :, None] * lg[:, None, None]) * lanes
    kd = jnp.exp((c - 1.0 - idx)[None, :, None] * lg[:, None, None]) * lanes
    cd = jnp.exp(c * lg)[:, None, None] * lanes
    return dm, qd, kd, cd


def _retention(rq, rk, rv, rg, s0, chunk):
    b, t, w = rq.shape
    c = min(chunk, t)
    dm, qd, kd, cd = _retention_tables(c)
    tile = pl.BlockSpec((1, c, w), lambda i, n: (i, n, 0))
    st = pl.BlockSpec((1, RET_HEADS, RET_HEAD_DIM, RET_HEAD_DIM), lambda i, n: (i, 0, 0, 0))
    const = lambda a: pl.BlockSpec(a.shape, lambda i, n: (0,) * a.ndim)
    return pl.pallas_call(
        _ret_body,
        grid=(b, t // c),
        in_specs=[tile, tile, tile, tile, st, const(dm), const(qd), const(kd), const(cd)],
        out_specs=[tile, st],
        out_shape=[jax.ShapeDtypeStruct((b, t, w), BF16),
                   jax.ShapeDtypeStruct(s0.shape, F32)],
        scratch_shapes=[pltpu.VMEM((RET_HEADS, RET_HEAD_DIM, RET_HEAD_DIM), F32)],
        compiler_params=_cparams(("parallel", "arbitrary")),
        name="retention",
    )(rq, rk, rv, rg, s0, dm, qd, kd, cd)


def _rope_tables(pos):
    half = RET_HEAD_DIM // 2
    inv_freq = ROPE_BASE ** (-jnp.arange(half, dtype=F32) / half)
    ang = pos.astype(F32)[:, None] * inv_freq[None, :]
    cos, sin = jnp.cos(ang), jnp.sin(ang)
    return jnp.concatenate([cos, cos], axis=-1), jnp.concatenate([-sin, sin], axis=-1)


def _block_diag_ones(width):
    i = lax.broadcasted_iota(jnp.int32, (width, width), 0) // SB_HEAD_DIM
    j = lax.broadcasted_iota(jnp.int32, (width, width), 1) // SB_HEAD_DIM
    return (i == j).astype(BF16)


def _layer(x, mod, pos, past, w, *, bs, tt):
    s, t, d = x.shape
    x = _ffn(x, mod, w["norm_ffn1"], w["ffn1_w_up"], w["ffn1_w_down"], bs=bs, tt=tt)
    cos, sin = _rope_tables(pos)
    q, k_f, k_b, v_f, v_b, rq, rk, rv, rg = _mix_proj(
        x, mod, w["norm_mix"], w["w_in"], w["sb_q_gain"], w["sb_k_gain"], cos, sin,
        _block_diag_ones(2 * LANES), bs=bs, tt=tt)
    if past is None:
        sb_o = _stick_break_prompt(q, k_b, v_b)
        s0 = jnp.zeros((s, RET_HEADS, RET_HEAD_DIM, RET_HEAD_DIM), F32)
        chunk = PROMPT_RET_CHUNK
    else:
        k_past, v_past, s0 = past
        sb_o = _stick_break_decode(q, k_b, v_b, jnp.transpose(k_past, (0, 2, 3, 1)),
                                   jnp.transpose(v_past, (0, 2, 3, 1)))
        chunk = t
    ret_o, state = _retention(rq, rk, rv, rg, s0, chunk)
    x = _ffn(x, mod, w["norm_ffn2"], w["ffn2_w_up"], w["ffn2_w_down"], bs=bs, tt=tt,
             mix=(sb_o, ret_o, w["w_out"]))
    heads = (s, t, SB_HEADS, SB_HEAD_DIM)
    return x, k_f.reshape(heads), v_f.reshape(heads), state


def kernel(x_prompt, x_sample, cache_sb_k, cache_sb_v, state_ret, c_prompt, c_sample, w_ada, b_ada, norm_ffn1, norm_mix, norm_ffn2, ffn1_w_up, ffn1_w_down, w_in, sb_q_gain, sb_k_gain, w_out, ffn2_w_up, ffn2_w_down):
    depth = w_ada.shape[0]
    bp, tp, d = x_prompt.shape
    bd, td, _ = x_sample.shape
    past_len = cache_sb_k.shape[2]
    pos_p = jnp.arange(tp, dtype=jnp.int32)
    pos_s = past_len + jnp.arange(td, dtype=jnp.int32)
    c_all = jnp.concatenate([c_prompt, c_sample], axis=0)
    tt_p = min(512, tp)
    bs_s = max(1, min(bd, 512 // td))
    assert tp % tt_p == 0 and bd % bs_s == 0

    y_p, y_s = x_prompt, x_sample
    outs = [[] for _ in range(6)]
    for l in range(depth):
        w = {
            "norm_ffn1": norm_ffn1[l][None], "norm_mix": norm_mix[l][None],
            "norm_ffn2": norm_ffn2[l][None],
            "ffn1_w_up": ffn1_w_up[l].astype(BF16), "ffn1_w_down": ffn1_w_down[l].astype(BF16),
            "ffn2_w_up": ffn2_w_up[l].astype(BF16), "ffn2_w_down": ffn2_w_down[l].astype(BF16),
            "w_in": w_in[l].astype(BF16), "w_out": w_out[l].astype(BF16),
            "sb_q_gain": jnp.tile(sb_q_gain[l], SB_HEADS)[None],
            "sb_k_gain": jnp.tile(sb_k_gain[l], SB_HEADS)[None],
        }
        mod = _modulation(c_all, w_ada[l], b_ada[l]).reshape(bp + bd, N_MOD, d)
        y_p, kp, vp, sp = _layer(y_p, mod[:bp], pos_p, None, w, bs=1, tt=tt_p)
        y_s, ks, vs, ss = _layer(y_s, mod[bp:], pos_s,
                                 (cache_sb_k[l], cache_sb_v[l], state_ret[l]), w,
                                 bs=bs_s, tt=td)
        for lst, a in zip(outs, (kp, vp, sp, ks, vs, ss)):
            lst.append(a)
    return (y_p, y_s) + tuple(jnp.stack(a) for a in outs)
```

```python
import functools

import jax
import jax.numpy as jnp
from jax import lax
from jax.experimental import pallas as pl
from jax.experimental.pallas import tpu as pltpu

F32 = jnp.float32
BF16 = jnp.bfloat16

N_MOD = 9
NORM_EPS = 1e-6
MACARON_WEIGHT = 0.5
ROPE_BASE = 10000.0
SB_HEADS = 8
SB_HEAD_DIM = 64
SB_WIDTH = SB_HEADS * SB_HEAD_DIM
RET_HEADS = 4
RET_HEAD_DIM = 128
RET_WIDTH = RET_HEADS * RET_HEAD_DIM

LANES = 128
KEY_BLOCK = LANES
HEADS_PER_TILE = LANES // SB_HEAD_DIM
LOG2_E = 1.4426950408889634
F32_EXP2_ZERO = -126.0
SOFTPLUS_CLAMP = 64.0
FIRST_SWEEP_BLOCKS = 3
PROMPT_RET_CHUNK = 256
DECODE_CHUNK = 256
VMEM_LIMIT = 56 * 1024 * 1024


def _cparams(sem):
    return pltpu.CompilerParams(dimension_semantics=sem, vmem_limit_bytes=VMEM_LIMIT)


def _resident(shape):
    nd = len(shape)
    return pl.BlockSpec(shape, lambda *_: (0,) * nd, pipeline_mode=pl.Buffered(1))


def _silu(x):
    return x * (1.0 / (1.0 + jnp.exp(-x)))


def _mod_body(c_ref, w_ref, b_ref, o_ref):
    c = c_ref[...]
    o_ref[...] = jnp.dot(_silu(c).astype(BF16), w_ref[...].astype(BF16),
                         preferred_element_type=F32) + b_ref[...]


def _modulation(c, w_ada, b_ada):
    s, d = c.shape
    n = w_ada.shape[1]
    tn = d
    return pl.pallas_call(
        _mod_body,
        grid=(n // tn,),
        in_specs=[pl.BlockSpec((s, d), lambda j: (0, 0)),
                  pl.BlockSpec((d, tn), lambda j: (0, j)),
                  pl.BlockSpec((1, tn), lambda j: (0, j))],
        out_specs=pl.BlockSpec((s, tn), lambda j: (0, j)),
        out_shape=jax.ShapeDtypeStruct((s, n), F32),
        compiler_params=_cparams(("arbitrary",)),
        name="modulation",
    )(c, w_ada, b_ada.reshape(1, n))


def _ada_rms(x, gain, shift, scale):
    inv = lax.rsqrt(jnp.mean(x * x, axis=-1, keepdims=True) + NORM_EPS)
    return (x * inv * gain) * (1.0 + scale) + shift


def _ffn_body(*refs, with_mix, n_fc):
    if with_mix:
        x_ref, sb_ref, rt_ref, mod_ref, gain_ref, wout_ref, wup_ref, wdn_ref, o_ref = refs
    else:
        x_ref, mod_ref, gain_ref, wup_ref, wdn_ref, o_ref = refs
    bs, tt, d = x_ref.shape
    ff = wdn_ref.shape[0]
    fc = ff // n_fc
    x = x_ref[...]
    mod = mod_ref[...]
    if with_mix:
        sb = sb_ref[...].reshape(bs * tt, SB_WIDTH)
        rt = rt_ref[...].reshape(bs * tt, RET_WIDTH)
        mixed = (jnp.dot(sb, wout_ref[:SB_WIDTH, :], preferred_element_type=F32)
                 + jnp.dot(rt, wout_ref[SB_WIDTH:, :], preferred_element_type=F32))
        x = x + mod[:, 5:6, :] * mixed.reshape(bs, tt, d)
        shift, scale, gate_mod = mod[:, 6:7, :], mod[:, 7:8, :], mod[:, 8:9, :]
    else:
        shift, scale, gate_mod = mod[:, 0:1, :], mod[:, 1:2, :], mod[:, 2:3, :]
    h = _ada_rms(x, gain_ref[...], shift, scale).astype(BF16).reshape(bs * tt, d)
    acts = []
    for c in range(n_fc):
        gate = jnp.dot(h, wup_ref[:, c * fc:(c + 1) * fc], preferred_element_type=F32)
        up = jnp.dot(h, wup_ref[:, ff + c * fc:ff + (c + 1) * fc], preferred_element_type=F32)
        acts.append((_silu(gate) * up).astype(BF16))
    acc = jnp.dot(jnp.concatenate(acts, axis=1), wdn_ref[...], preferred_element_type=F32)
    o_ref[...] = x + MACARON_WEIGHT * gate_mod * acc.reshape(bs, tt, d)


def _ffn(x, mod, gain, w_up, w_down, *, bs, tt, mix=None):
    s, t, d = x.shape
    ff = w_down.shape[0]
    n_fc = 2 if ff % (2 * LANES) == 0 else 1
    grid = (s // bs, t // tt)
    tile = lambda w: pl.BlockSpec((bs, tt, w), lambda i, j: (i, j, 0))
    mod_spec = pl.BlockSpec((bs, N_MOD, d), lambda i, j: (i, 0, 0))
    if mix is None:
        args = (x, mod, gain, w_up, w_down)
        specs = [tile(d), mod_spec, _resident(gain.shape), _resident(w_up.shape),
                 _resident(w_down.shape)]
    else:
        sb, rt, w_out = mix
        args = (x, sb, rt, mod, gain, w_out, w_up, w_down)
        specs = [tile(d), tile(SB_WIDTH), tile(RET_WIDTH), mod_spec, _resident(gain.shape),
                 _resident(w_out.shape), _resident(w_up.shape), _resident(w_down.shape)]
    return pl.pallas_call(
        functools.partial(_ffn_body, with_mix=mix is not None, n_fc=n_fc),
        grid=grid,
        in_specs=specs,
        out_specs=tile(d),
        out_shape=jax.ShapeDtypeStruct((s, t, d), F32),
        compiler_params=_cparams(("parallel", "parallel")),
        name="ffn_mix" if mix is not None else "ffn",
    )(*args)


def _proj_body(x_ref, mod_ref, gain_ref, win_ref, qg_ref, kg_ref, cos_ref, sin_ref, bd_ref,
               q_ref, kf_ref, ks_ref, vf_ref, vs_ref, rq_ref, rk_ref, rv_ref, rg_ref):
    bs, tt, d = x_ref.shape
    m = bs * tt
    mod = mod_ref[...]
    h = _ada_rms(x_ref[...], gain_ref[...], mod[:, 3:4, :], mod[:, 4:5, :])
    h = h.astype(BF16).reshape(m, d)

    def cols(a, w):
        return jnp.dot(h, win_ref[:, a:a + w], preferred_element_type=F32)

    qk = cols(0, 2 * SB_WIDTH)
    sq = (qk * qk).astype(BF16)
    bw = bd_ref.shape[0]
    ms = jnp.concatenate(
        [jnp.dot(sq[:, j * bw:(j + 1) * bw], bd_ref[...], preferred_element_type=F32)
         for j in range(2 * SB_WIDTH // bw)], axis=1) * (1.0 / SB_HEAD_DIM)
    qk = qk * lax.rsqrt(ms + NORM_EPS)
    qn = qk[:, :SB_WIDTH] * qg_ref[...]
    kn = qk[:, SB_WIDTH:] * kg_ref[...]
    q_ref[...] = (qn * (LOG2_E * SB_HEAD_DIM ** -0.5)).astype(BF16).reshape(bs, tt, SB_WIDTH)
    kf_ref[...] = kn.reshape(bs, tt, SB_WIDTH)
    v = cols(2 * SB_WIDTH, SB_WIDTH)
    vf_ref[...] = v.reshape(bs, tt, SB_WIDTH)
    first_head = lax.broadcasted_iota(jnp.int32, (m, SB_WIDTH), 1) % LANES < SB_HEAD_DIM
    for src, dst in ((kn, ks_ref), (v, vs_ref)):
        for half, keep in enumerate((first_head, jnp.logical_not(first_head))):
            dst[:, half] = jnp.where(keep, src, 0.0).astype(BF16).reshape(bs, tt, SB_WIDTH)

    cos = cos_ref[...][None]
    sin = sin_ref[...][None]

    def rope(a, scale):
        y = cols(a, RET_WIDTH)
        outs = []
        for hh in range(RET_HEADS):
            yh = y[:, hh * RET_HEAD_DIM:(hh + 1) * RET_HEAD_DIM]
            sw = pltpu.roll(yh, RET_HEAD_DIM // 2, axis=1)
            r = yh.reshape(bs, tt, RET_HEAD_DIM) * cos + sw.reshape(bs, tt, RET_HEAD_DIM) * sin
            outs.append(r)
        r = jnp.concatenate(outs, axis=-1)
        if scale != 1.0:
            r = r * scale
        return r.astype(BF16)

    base = 3 * SB_WIDTH
    rq_ref[...] = rope(base, 1.0)
    rk_ref[...] = rope(base + RET_WIDTH, RET_HEAD_DIM ** -0.5)
    rv_ref[...] = cols(base + 2 * RET_WIDTH, RET_WIDTH).astype(BF16).reshape(bs, tt, RET_WIDTH)
    rg_ref[...] = _silu(cols(base + 3 * RET_WIDTH, RET_WIDTH)).astype(BF16).reshape(bs, tt, RET_WIDTH)


def _mix_proj(x, mod, gain, w_in, qg, kg, cos, sin, bd, *, bs, tt):
    s, t, d = x.shape
    grid = (s // bs, t // tt)
    tile = lambda w: pl.BlockSpec((bs, tt, w), lambda i, j: (i, j, 0))
    tab = pl.BlockSpec((tt, RET_HEAD_DIM), lambda i, j: (j, 0))
    w = SB_WIDTH
    out = lambda dt: jax.ShapeDtypeStruct((s, t, w), dt)
    split = pl.BlockSpec((bs, 2, tt, w), lambda i, j: (i, 0, j, 0))
    out_split = jax.ShapeDtypeStruct((s, 2, t, w), BF16)
    return pl.pallas_call(
        _proj_body,
        grid=grid,
        in_specs=[tile(d), pl.BlockSpec((bs, N_MOD, d), lambda i, j: (i, 0, 0)),
                  _resident(gain.shape), _resident(w_in.shape), _resident(qg.shape),
                  _resident(kg.shape), tab, tab, _resident(bd.shape)],
        out_specs=[tile(w), tile(w), split, tile(w), split] + [tile(w)] * 4,
        out_shape=[out(BF16), out(F32), out_split, out(F32), out_split,
                   out(BF16), out(BF16), out(BF16), out(BF16)],
        compiler_params=_cparams(("parallel", "parallel")),
        name="mix_proj",
    )(x, mod, gain, w_in, qg, kg, cos, sin, bd)


def _split_heads(blk, axis):
    head_axis = 1 - axis
    pos = lax.broadcasted_iota(jnp.int32, blk.shape, head_axis)
    zero = jnp.zeros_like(blk)
    return jnp.concatenate([jnp.where(pos < SB_HEAD_DIM, blk, zero),
                            jnp.where(pos >= SB_HEAD_DIM, blk, zero)], axis=axis)


def _sb_sweep(qs, groups, tri_ones2, acc_ref, car_ref, *, rows, init):
    tk = KEY_BLOCK
    nt = (((1,), (1,)), ((), ()))
    n = len(qs) * HEADS_PER_TILE * rows
    zs = []
    for k_ops, _, transposed, _ in groups:
        for q, k_op in zip(qs, k_ops):
            if transposed:
                z = jnp.dot(q, k_op, preferred_element_type=F32)
            else:
                z = lax.dot_general(q, k_op, nt, preferred_element_type=F32)
            zs += [z[:, :tk], z[:, tk:]]
    z = jnp.concatenate(zs, axis=0)
    rest = jnp.maximum(jnp.log(1.0 + jnp.exp2(jnp.minimum(z, SOFTPLUS_CLAMP))) * LOG2_E, z)
    log_beta = z - rest
    row = lax.broadcasted_iota(jnp.int32, (n, tk), 0) % rows
    col = lax.broadcasted_iota(jnp.int32, (n, tk), 1)
    mask = col < row
    rest = jnp.concatenate(
        [jnp.where(mask, rest[g * n:(g + 1) * n], 0.0) if grp[3] else rest[g * n:(g + 1) * n]
         for g, grp in enumerate(groups)], axis=0)
    hi = rest.astype(BF16)
    lo = (rest - hi.astype(F32)).astype(BF16)
    cs = jnp.dot(jnp.concatenate([hi, lo], axis=1), tri_ones2, preferred_element_type=F32)
    run = None if init else car_ref[...]
    ws = []
    for g, grp in enumerate(groups):
        between = cs[g * n:(g + 1) * n, :tk]
        if run is not None:
            between = between + run
        w = jnp.exp2(log_beta[g * n:(g + 1) * n] + between)
        if grp[3]:
            w = jnp.where(mask, w, 0.0)
        ws.append(w.astype(BF16))
        sums = cs[g * n:(g + 1) * n, tk:]
        run = sums if run is None else run + sums
    car_ref[...] = run
    for p in range(len(qs)):
        pv = None
        for w, (_, v_ops, transposed, _) in zip(ws, groups):
            w2 = jnp.concatenate([w[(2 * p) * rows:(2 * p + 1) * rows],
                                  w[(2 * p + 1) * rows:(2 * p + 2) * rows]], axis=1)
            if transposed:
                d = lax.dot_general(w2, v_ops[p], nt, preferred_element_type=F32)
            else:
                d = jnp.dot(w2, v_ops[p], preferred_element_type=F32)
            pv = d if pv is None else pv + d
        if init:
            acc_ref[p * rows:(p + 1) * rows, :] = pv
        else:
            acc_ref[p * rows:(p + 1) * rows, :] += pv


def _sb_live(car_ref):
    m = car_ref[...]
    while m.shape[0] > 8:
        half = m.shape[0] // 2
        m = jnp.maximum(m[:half], m[half:])
    return (jnp.max(m) > F32_EXP2_ZERO).astype(jnp.int32)


def _sb_store(o_ref, acc_ref, rows, tiles):
    for p in range(tiles):
        o_ref[0, :, p * LANES:(p + 1) * LANES] = acc_ref[p * rows:(p + 1) * rows, :].astype(o_ref.dtype)


def _sb_prompt_body(q_ref, k_ref, v_ref, to_ref, o_ref, acc_ref, car_ref):
    qi = pl.program_id(1)
    rows = q_ref.shape[1]
    tk = KEY_BLOCK
    tiles = q_ref.shape[2] // LANES
    lanes = [slice(p * LANES, (p + 1) * LANES) for p in range(tiles)]
    tri_ones2 = to_ref[...]
    qs = [q_ref[0, :, sl] for sl in lanes]

    def block(kb, masked):
        keys = pl.ds(pl.multiple_of(kb * tk, tk), tk)
        ops = lambda ref: [jnp.concatenate([ref[0, 0, keys, sl], ref[0, 1, keys, sl]], axis=0)
                           for sl in lanes]
        return ops(k_ref), ops(v_ref), False, masked

    sweep = functools.partial(_sb_sweep, qs, tri_ones2=tri_ones2, acc_ref=acc_ref,
                              car_ref=car_ref, rows=rows)

    @pl.when(qi == 0)
    def _():
        sweep([block(qi, True)], init=True)

    @pl.when(qi == 1)
    def _():
        sweep([block(qi, True), block(qi - 1, False)], init=True)

    @pl.when(qi > 1)
    def _():
        sweep([block(qi, True), block(qi - 1, False), block(qi - 2, False)], init=True)

    def body(c):
        sweep([block(c[0], False)], init=False)
        return c[0] - 1, _sb_live(car_ref)

    lax.while_loop(lambda c: jnp.logical_and(c[0] >= 0, c[1] > 0), body,
                   (qi - FIRST_SWEEP_BLOCKS, _sb_live(car_ref)))
    _sb_store(o_ref, acc_ref, rows, tiles)


def _sb_decode_body(q_ref, kn_ref, vn_ref, kt_hbm, vt_hbm, to_ref, o_ref,
                    acc_ref, car_ref, kbuf, vbuf, sem):
    b = pl.program_id(0)
    nb = pl.num_programs(0)
    slot = b % 2
    rows = q_ref.shape[1]
    tk = KEY_BLOCK
    tiles = q_ref.shape[2] // LANES
    lanes = [slice(p * LANES, (p + 1) * LANES) for p in range(tiles)]
    chunk = kbuf.shape[3]
    subs = chunk // tk
    n_chunks = kt_hbm.shape[3] // chunk
    tri_ones2 = to_ref[...]
    qs = [q_ref[0, :, sl] for sl in lanes]

    def copies(stream, ci, to_slot):
        start = pl.multiple_of(ci * chunk, chunk)
        return (pltpu.make_async_copy(kt_hbm.at[stream, :, :, pl.ds(start, chunk)],
                                      kbuf.at[to_slot], sem.at[0, to_slot]),
                pltpu.make_async_copy(vt_hbm.at[stream, :, :, pl.ds(start, chunk)],
                                      vbuf.at[to_slot], sem.at[1, to_slot]))

    @pl.when(b == 0)
    def _():
        for cp in copies(b, n_chunks - 1, slot):
            cp.start()

    @pl.when(b + 1 < nb)
    def _():
        for cp in copies(b + 1, n_chunks - 1, 1 - slot):
            cp.start()

    for cp in copies(b, n_chunks - 1, slot):
        cp.wait()

    def cached(sub):
        def ops(buf):
            return [_split_heads(
                buf[slot, pl.ds(p * HEADS_PER_TILE, HEADS_PER_TILE), :, pl.ds(sub * tk, tk)]
                .reshape(LANES, tk).astype(BF16), 1) for p in range(tiles)]
        return ops(kbuf), ops(vbuf), True, False

    def new_keys():
        pad = jnp.zeros((tk - rows, LANES), BF16)
        ops = lambda ref: [jnp.concatenate([ref[0, 0, :, sl], pad, ref[0, 1, :, sl], pad], axis=0)
                           for sl in lanes]
        return ops(kn_ref), ops(vn_ref), False, True

    sweep = functools.partial(_sb_sweep, qs, tri_ones2=tri_ones2, acc_ref=acc_ref,
                              car_ref=car_ref, rows=rows)

    def older(first_sub):
        for sub in reversed(range(first_sub + 1)):
            pl.when(_sb_live(car_ref) > 0)(lambda sub=sub: sweep([cached(sub)], init=False))

    sweep([new_keys(), cached(subs - 1)], init=True)
    older(subs - 2)

    def body(c):
        for cp in copies(b, c[0], slot):
            cp.start()
        for cp in copies(b, c[0], slot):
            cp.wait()
        older(subs - 1)
        return c[0] - 1, _sb_live(car_ref)

    lax.while_loop(lambda c: jnp.logical_and(c[0] >= 0, c[1] > 0), body,
                   (jnp.int32(n_chunks - 2), _sb_live(car_ref)))
    _sb_store(o_ref, acc_ref, rows, tiles)


def _tri_ones2():
    j = lax.broadcasted_iota(jnp.int32, (2 * KEY_BLOCK, 2 * KEY_BLOCK), 0) % KEY_BLOCK
    s = lax.broadcasted_iota(jnp.int32, (2 * KEY_BLOCK, 2 * KEY_BLOCK), 1)
    return -jnp.logical_or(j > s, s >= KEY_BLOCK).astype(BF16)


def _sb_scratch(rows, width):
    tiles = width // LANES
    return [pltpu.VMEM((tiles * rows, LANES), F32),
            pltpu.VMEM((tiles * HEADS_PER_TILE * rows, LANES), F32)]


def _stick_break_prompt(q, k, v):
    b, t, w = q.shape
    rows = KEY_BLOCK
    qspec = pl.BlockSpec((1, rows, w), lambda i, j: (i, j, 0))
    full = pl.BlockSpec((1, 2, t, w), lambda i, j: (i, 0, 0, 0), pipeline_mode=pl.Buffered(1))
    to = _tri_ones2()
    return pl.pallas_call(
        _sb_prompt_body,
        grid=(b, t // rows),
        in_specs=[qspec, full, full, _resident(to.shape)],
        out_specs=qspec,
        out_shape=jax.ShapeDtypeStruct((b, t, w), BF16),
        scratch_shapes=_sb_scratch(rows, w),
        compiler_params=_cparams(("parallel", "arbitrary")),
        name="stick_break_prompt",
    )(q, k, v, to)


def _stick_break_decode(q, k_new, v_new, kt_past, vt_past):
    b, rows, w = q.shape
    past = kt_past.shape[3]
    chunk = min(DECODE_CHUNK, past)
    assert past % chunk == 0 and chunk % KEY_BLOCK == 0 and rows <= KEY_BLOCK
    new = pl.BlockSpec((1, rows, w), lambda i: (i, 0, 0))
    new_split = pl.BlockSpec((1, 2, rows, w), lambda i: (i, 0, 0, 0))
    hbm = pl.BlockSpec(memory_space=pl.ANY)
    to = _tri_ones2()
    return pl.pallas_call(
        _sb_decode_body,
        grid=(b,),
        in_specs=[new, new_split, new_split, hbm, hbm, _resident(to.shape)],
        out_specs=new,
        out_shape=jax.ShapeDtypeStruct((b, rows, w), BF16),
        scratch_shapes=_sb_scratch(rows, w) + [
            pltpu.VMEM((2, SB_HEADS, SB_HEAD_DIM, chunk), F32),
            pltpu.VMEM((2, SB_HEADS, SB_HEAD_DIM, chunk), F32),
            pltpu.SemaphoreType.DMA((2, 2))],
        compiler_params=_cparams(("arbitrary",)),
        name="stick_break_decode",
    )(q, k_new, v_new, kt_past, vt_past, to)


def _ret_body(q_ref, k_ref, v_ref, g_ref, s0_ref, dm_ref, qd_ref, kd_ref, cd_ref,
              o_ref, so_ref, s_scr):
    n = pl.program_id(1)

    @pl.when(n == 0)
    def _():
        s_scr[...] = s0_ref[0]

    for h in range(RET_HEADS):
        sl = slice(h * RET_HEAD_DIM, (h + 1) * RET_HEAD_DIM)
        q = q_ref[0, :, sl]
        k = k_ref[0, :, sl]
        v = v_ref[0, :, sl]
        state = s_scr[h]
        scores = lax.dot_general(q, k, (((1,), (1,)), ((), ())),
                                 preferred_element_type=F32) * dm_ref[h]
        intra = jnp.dot(scores.astype(BF16), v, preferred_element_type=F32)
        q_in = (q.astype(F32) * qd_ref[h]).astype(BF16)
        cross = jnp.dot(q_in, state.astype(BF16), preferred_element_type=F32)
        k_out = (k.astype(F32) * kd_ref[h]).astype(BF16)
        kv = lax.dot_general(k_out, v, (((0,), (0,)), ((), ())), preferred_element_type=F32)
        s_scr[h] = cd_ref[h] * state + kv
        o = intra + cross
        oc = o - jnp.mean(o, axis=-1, keepdims=True)
        gn = oc * lax.rsqrt(jnp.mean(oc * oc, axis=-1, keepdims=True) + NORM_EPS)
        o_ref[0, :, sl] = (g_ref[0, :, sl].astype(F32) * gn).astype(o_ref.dtype)

    @pl.when(n == pl.num_programs(1) - 1)
    def _():
        so_ref[0] = s_scr[...]


def _retention_tables(c):
    lg = jnp.log1p(-jnp.exp2(-5.0 - jnp.arange(RET_HEADS, dtype=F32)))
    idx = jnp.arange(c, dtype=F32)
    diff = idx[:, None] - idx[None, :]
    dm = jnp.where(diff[None] >= 0, jnp.exp(jnp.maximum(diff, 0.0)[None] * lg[:, None, None]), 0.0)
    lanes = jnp.ones((1, 1, RET_HEAD_DIM), F32)
    qd = jnp.exp((idx + 1.0)[None, :, None] * lg[:, None, None]) * lanes
    kd = jnp.exp((c - 1.0 - idx)[None, :, None] * lg[:, None, None]) * lanes
    cd = jnp.exp(c * lg)[:, None, None] * lanes
    return dm, qd, kd, cd


def _retention(rq, rk, rv, rg, s0, chunk):
    b, t, w = rq.shape
    c = min(chunk, t)
    dm, qd, kd, cd = _retention_tables(c)
    tile = pl.BlockSpec((1, c, w), lambda i, n: (i, n, 0))
    st = pl.BlockSpec((1, RET_HEADS, RET_HEAD_DIM, RET_HEAD_DIM), lambda i, n: (i, 0, 0, 0))
    const = lambda a: pl.BlockSpec(a.shape, lambda i, n: (0,) * a.ndim)
    return pl.pallas_call(
        _ret_body,
        grid=(b, t // c),
        in_specs=[tile, tile, tile, tile, st, const(dm), const(qd), const(kd), const(cd)],
        out_specs=[tile, st],
        out_shape=[jax.ShapeDtypeStruct((b, t, w), BF16),
                   jax.ShapeDtypeStruct(s0.shape, F32)],
        scratch_shapes=[pltpu.VMEM((RET_HEADS, RET_HEAD_DIM, RET_HEAD_DIM), F32)],
        compiler_params=_cparams(("parallel", "arbitrary")),
        name="retention",
    )(rq, rk, rv, rg, s0, dm, qd, kd, cd)


def _rope_tables(pos):
    half = RET_HEAD_DIM // 2
    inv_freq = ROPE_BASE ** (-jnp.arange(half, dtype=F32) / half)
    ang = pos.astype(F32)[:, None] * inv_freq[None, :]
    cos, sin = jnp.cos(ang), jnp.sin(ang)
    return jnp.concatenate([cos, cos], axis=-1), jnp.concatenate([-sin, sin], axis=-1)


def _block_diag_ones(width):
    i = lax.broadcasted_iota(jnp.int32, (width, width), 0) // SB_HEAD_DIM
    j = lax.broadcasted_iota(jnp.int32, (width, width), 1) // SB_HEAD_DIM
    return (i == j).astype(BF16)


def _layer(x, mod, pos, past, w, *, bs, tt):
    s, t, d = x.shape
    x = _ffn(x, mod, w["norm_ffn1"], w["ffn1_w_up"], w["ffn1_w_down"], bs=bs, tt=tt)
    cos, sin = _rope_tables(pos)
    q, k_f, k_b, v_f, v_b, rq, rk, rv, rg = _mix_proj(
        x, mod, w["norm_mix"], w["w_in"], w["sb_q_gain"], w["sb_k_gain"], cos, sin,
        _block_diag_ones(2 * LANES), bs=bs, tt=tt)
    if past is None:
        sb_o = _stick_break_prompt(q, k_b, v_b)
        s0 = jnp.zeros((s, RET_HEADS, RET_HEAD_DIM, RET_HEAD_DIM), F32)
        chunk = PROMPT_RET_CHUNK
    else:
        k_past, v_past, s0 = past
        sb_o = _stick_break_decode(q, k_b, v_b, jnp.transpose(k_past, (0, 2, 3, 1)),
                                   jnp.transpose(v_past, (0, 2, 3, 1)))
        chunk = t
    ret_o, state = _retention(rq, rk, rv, rg, s0, chunk)
    x = _ffn(x, mod, w["norm_ffn2"], w["ffn2_w_up"], w["ffn2_w_down"], bs=bs, tt=tt,
             mix=(sb_o, ret_o, w["w_out"]))
    heads = (s, t, SB_HEADS, SB_HEAD_DIM)
    return x, k_f.reshape(heads), v_f.reshape(heads), state


def kernel(x_prompt, x_sample, cache_sb_k, cache_sb_v, state_ret, c_prompt, c_sample, w_ada, b_ada, norm_ffn1, norm_mix, norm_ffn2, ffn1_w_up, ffn1_w_down, w_in, sb_q_gain, sb_k_gain, w_out, ffn2_w_up, ffn2_w_down):
    depth = w_ada.shape[0]
    bp, tp, d = x_prompt.shape
    bd, td, _ = x_sample.shape
    past_len = cache_sb_k.shape[2]
    pos_p = jnp.arange(tp, dtype=jnp.int32)
    pos_s = past_len + jnp.arange(td, dtype=jnp.int32)
    c_all = jnp.concatenate([c_prompt, c_sample], axis=0)
    tt_p = min(512, tp)
    bs_s = max(1, min(bd, 512 // td))
    assert tp % tt_p == 0 and bd % bs_s == 0

    y_p, y_s = x_prompt, x_sample
    outs = [[] for _ in range(6)]
    for l in range(depth):
        w = {
            "norm_ffn1": norm_ffn1[l][None], "norm_mix": norm_mix[l][None],
            "norm_ffn2": norm_ffn2[l][None],
            "ffn1_w_up": ffn1_w_up[l].astype(BF16), "ffn1_w_down": ffn1_w_down[l].astype(BF16),
            "ffn2_w_up": ffn2_w_up[l].astype(BF16), "ffn2_w_down": ffn2_w_down[l].astype(BF16),
            "w_in": w_in[l].astype(BF16), "w_out": w_out[l].astype(BF16),
            "sb_q_gain": jnp.tile(sb_q_gain[l], SB_HEADS)[None],
            "sb_k_gain": jnp.tile(sb_k_gain[l], SB_HEADS)[None],
        }
        mod = _modulation(c_all, w_ada[l], b_ada[l]).reshape(bp + bd, N_MOD, d)
        y_p, kp, vp, sp = _layer(y_p, mod[:bp], pos_p, None, w, bs=1, tt=tt_p)
        y_s, ks, vs, ss = _layer(y_s, mod[bp:], pos_s,
                                 (cache_sb_k[l], cache_sb_v[l], state_ret[l]), w,
                                 bs=bs_s, tt=td)
        for lst, a in zip(outs, (kp, vp, sp, ks, vs, ss)):
            lst.append(a)
    return (y_p, y_s) + tuple(jnp.stack(a) for a in outs)
```
